```python
import jax, jax.numpy as jnp
from jax import lax
import numpy as np

D_MODEL = 1024
BATCH = 2
SEQ = 8192
DEPTH = 1

N_HEADS = 8
HEAD_DIM = 64
ATTN_WIDTH = N_HEADS * HEAD_DIM
N_IDX_HEADS = 8
IDX_DIM = 64
TOPK_MAX = 256
Q_BLOCK = 128
CONV_CH = 512
CONV_KERNEL = 31
D_FF = -(-8 * D_MODEL // (3 * 256)) * 256
PLE_DIM = 256
EPS = 1e-6

IN_SIZES = (ATTN_WIDTH, ATTN_WIDTH, ATTN_WIDTH, N_IDX_HEADS * IDX_DIM, IDX_DIM,
            N_IDX_HEADS, 2 * CONV_CH, D_MODEL, D_MODEL)
IN_WIDTH = sum(IN_SIZES)

kernel_name = "hybrid_dsa_conformer_gated_block"


def _split_points():
    pts, acc = [], 0
    for s in IN_SIZES[:-1]:
        acc += s
        pts.append(acc)
    return pts


def rms_norm(x, g):
    xf = x.astype(jnp.float32)
    y = xf * lax.rsqrt(jnp.mean(xf * xf, axis=-1, keepdims=True) + EPS)
    return (y * g.astype(jnp.float32)).astype(x.dtype)


def layer_norm(x, g, b):
    xf = x.astype(jnp.float32)
    mu = jnp.mean(xf, axis=-1, keepdims=True)
    var = jnp.mean(jnp.square(xf - mu), axis=-1, keepdims=True)
    y = (xf - mu) * lax.rsqrt(var + EPS)
    return (y * g.astype(jnp.float32) + b.astype(jnp.float32)).astype(x.dtype)


def dsa_attention(q, k, v, qi, ki, wi, topk):
    B, S, H, D = q.shape
    nb = S // Q_BLOCK
    key_pos = jnp.arange(S)
    q = q * (D ** -0.5)
    wi = wi * (N_IDX_HEADS ** -0.5)

    def block(n):
        t0 = n * Q_BLOCK
        qb = lax.dynamic_slice_in_dim(q, t0, Q_BLOCK, axis=1)
        qib = lax.dynamic_slice_in_dim(qi, t0, Q_BLOCK, axis=1)
        wib = lax.dynamic_slice_in_dim(wi, t0, Q_BLOCK, axis=1)
        qpos = t0 + jnp.arange(Q_BLOCK)
        causal = key_pos[None, :] <= qpos[:, None]
        rel = jax.nn.relu(jnp.einsum('bqhd,bsd->bqhs', qib, ki).astype(jnp.float32)
                          * (IDX_DIM ** -0.5))
        score = jnp.einsum('bqhs,bqh->bqs', rel, wib.astype(jnp.float32))
        score = jnp.where(causal[None], score, -jnp.inf)
        _, idx = lax.top_k(score, topk)
        ksel = jax.vmap(lambda kb, ib: kb[ib])(k, idx)
        vsel = jax.vmap(lambda vb, ib: vb[ib])(v, idx)
        logits = jnp.einsum('bqhd,bqkhd->bqhk', qb, ksel).astype(jnp.float32)
        valid = idx <= qpos[None, :, None]
        logits = jnp.where(valid[:, :, None, :], logits, -jnp.inf)
        probs = jax.nn.softmax(logits, axis=-1).astype(v.dtype)
        return jnp.einsum('bqhk,bqkhd->bqhd', probs, vsel)

    out = lax.map(block, jnp.arange(nb))
    return out.transpose(1, 0, 2, 3, 4).reshape(B, S, H * D)


def conformer_conv(u, conv_w, conv_b, ln_g, ln_b):
    a, g = jnp.split(u, 2, axis=-1)
    glu = a * jax.nn.sigmoid(g)
    y = lax.conv_general_dilated(
        glu, conv_w.astype(glu.dtype), window_strides=(1,),
        padding=[(CONV_KERNEL - 1, 0)],
        dimension_numbers=('NWC', 'WIO', 'NWC'),
        feature_group_count=CONV_CH)
    y = y + conv_b
    y = layer_norm(y, ln_g, ln_b)
    return jax.nn.silu(y)


def setup_inputs(seed: int = 0) -> dict:
    key = jax.random.key(seed)
    ks = jax.random.split(key, 24)
    f32 = jnp.float32

    def nrm(k, shape, fan_in):
        return jax.random.normal(k, shape, f32) * (fan_in ** -0.5)

    def gain(k, shape):
        return 1.0 + 0.01 * jax.random.normal(k, shape, f32)

    L = DEPTH
    return {
        "x": jax.random.normal(ks[0], (BATCH, SEQ, D_MODEL), f32),
        "p": jax.random.normal(ks[1], (DEPTH, BATCH, SEQ, PLE_DIM), f32),
        "g_mix": gain(ks[2], (L, D_MODEL)),
        "w_in": nrm(ks[3], (L, D_MODEL, IN_WIDTH), D_MODEL),
        "g_q": gain(ks[4], (L, HEAD_DIM)),
        "g_k": gain(ks[5], (L, HEAD_DIM)),
        "conv_w": nrm(ks[6], (L, CONV_KERNEL, 1, CONV_CH), CONV_KERNEL),
        "conv_b": 0.01 * jax.random.normal(ks[7], (L, CONV_CH), f32),
        "conv_ln_g": gain(ks[8], (L, CONV_CH)),
        "conv_ln_b": 0.01 * jax.random.normal(ks[9], (L, CONV_CH), f32),
        "w_attn_o": nrm(ks[10], (L, ATTN_WIDTH, D_MODEL), ATTN_WIDTH),
        "w_conv_o": nrm(ks[11], (L, CONV_CH, D_MODEL), CONV_CH),
        "w_out": nrm(ks[12], (L, D_MODEL, D_MODEL), D_MODEL),
        "g_ffn": gain(ks[13], (L, D_MODEL)),
        "w_ffn_gate": nrm(ks[14], (L, D_MODEL, D_FF), D_MODEL),
        "w_ffn_up": nrm(ks[15], (L, D_MODEL, D_FF), D_MODEL),
        "w_ffn_down": nrm(ks[16], (L, D_FF, D_MODEL), D_FF),
        "g_ple": gain(ks[17], (L, D_MODEL)),
        "w_ple_gate": nrm(ks[18], (L, D_MODEL, D_MODEL), D_MODEL),
        "w_ple_proj": nrm(ks[19], (L, PLE_DIM, D_MODEL), PLE_DIM),
    }


def reference(x, p, g_mix, w_in, g_q, g_k, conv_w, conv_b, conv_ln_g, conv_ln_b,
              w_attn_o, w_conv_o, w_out, g_ffn, w_ffn_gate, w_ffn_up, w_ffn_down,
              g_ple, w_ple_gate, w_ple_proj):
    B, S, _ = x.shape
    topk = min(TOPK_MAX, S // 4)
    pts = _split_points()
    for i in range(DEPTH):
        h = rms_norm(x, g_mix[i])
        proj = h @ w_in[i]
        q, k, v, qi, ki, wi, conv_in, gate_a, gate_c = jnp.split(proj, pts, axis=-1)
        q = rms_norm(q.reshape(B, S, N_HEADS, HEAD_DIM), g_q[i])
        k = rms_norm(k.reshape(B, S, N_HEADS, HEAD_DIM), g_k[i])
        v = v.reshape(B, S, N_HEADS, HEAD_DIM)
        qi = qi.reshape(B, S, N_IDX_HEADS, IDX_DIM)
        attn = dsa_attention(q, k, v, qi, ki, wi, topk)
        conv = conformer_conv(conv_in, conv_w[i], conv_b[i],
                              conv_ln_g[i], conv_ln_b[i])
        merged = (jax.nn.sigmoid(gate_a) * (attn @ w_attn_o[i])
                  + jax.nn.sigmoid(gate_c) * (conv @ w_conv_o[i]))
        x = x + merged @ w_out[i]
        hf = rms_norm(x, g_ffn[i])
        x = x + (jax.nn.silu(hf @ w_ffn_gate[i]) * (hf @ w_ffn_up[i])) @ w_ffn_down[i]
        hp = rms_norm(x, g_ple[i])
        x = x + jax.nn.sigmoid(hp @ w_ple_gate[i]) * (p[i] @ w_ple_proj[i])
    return x
```

```python
import functools
import math

import jax
import jax.numpy as jnp
from jax import lax
from jax.experimental import pallas as pl
from jax.experimental.pallas import tpu as pltpu

F32 = jnp.float32
BF16 = jnp.bfloat16
I32 = jnp.int32

N_HEADS = 8
HEAD_DIM = 64
ATTN_WIDTH = N_HEADS * HEAD_DIM
N_IDX_HEADS = 8
IDX_DIM = 64
TOPK_MAX = 256
CONV_CH = 512
CONV_KERNEL = 31
EPS = 1e-6

LANES = 128
CONV_HALO = 32
VMEM_LIMIT = 56 * 1024 * 1024

KEY_LOWEST = -2139095040
KEY_INF = 2139095040
MASK_VAL = -1e30


def _key_to_f32(k):
    bits = k ^ ((k >> 31) & 0x7FFFFFFF)
    return lax.bitcast_convert_type(bits, F32)


def _rms_h(x, g):
    ms = jnp.mean(x * x, axis=-1, keepdims=True)
    return (x * lax.rsqrt(ms + EPS) * g).astype(BF16)


def _split_dot(a, b):
    hi = a.astype(BF16)
    lo = (a - hi.astype(F32)).astype(BF16)
    return (jnp.dot(hi, b, preferred_element_type=F32)
            + jnp.dot(lo, b, preferred_element_type=F32))


def _split_dot_left(b, a):
    hi = a.astype(BF16)
    lo = (a - hi.astype(F32)).astype(BF16)
    return (jnp.dot(b, hi, preferred_element_type=F32)
            + jnp.dot(b, lo, preferred_element_type=F32))


def _proj_kernel(x_ref, gmix_ref, wq_ref, wkT_ref, wv_ref, wqi_ref, wkiT_ref, wwi_ref,
                 gq_ref, gk_ref, hd_ref,
                 q_out, kT_out, v_out, qi_out, kiT_out, w_out):
    tm = x_ref.shape[0]
    h = _rms_h(x_ref[...], gmix_ref[...])
    hd = hd_ref[...]

    lane = lax.broadcasted_iota(I32, (tm, LANES), 1)
    lo_half = lane < HEAD_DIM

    def expand_heads(y, out_ref):
        for pair in range(N_HEADS // 2):
            blk = y[:, pair * LANES:(pair + 1) * LANES]
            out_ref[:, (2 * pair) * LANES:(2 * pair + 1) * LANES] = jnp.where(lo_half, blk, 0.0).astype(BF16)
            out_ref[:, (2 * pair + 1) * LANES:(2 * pair + 2) * LANES] = jnp.where(lo_half, 0.0, blk).astype(BF16)

    q = jnp.dot(h, wq_ref[...], preferred_element_type=F32)
    ssq = _split_dot(q * q, hd)
    qn = q * lax.rsqrt(ssq * (1.0 / HEAD_DIM) + EPS) * gq_ref[...]
    expand_heads(qn, q_out)

    nt = (((1,), (1,)), ((), ()))
    kT = lax.dot_general(wkT_ref[...], h, nt, preferred_element_type=F32)
    ssk = _split_dot_left(hd, kT * kT)
    kT_out[0] = (kT * lax.rsqrt(ssk * (1.0 / HEAD_DIM) + EPS) * gk_ref[...]).astype(BF16)

    v_out[...] = jnp.dot(h, wv_ref[...], preferred_element_type=F32).astype(BF16)

    qi = jnp.dot(h, wqi_ref[...], preferred_element_type=F32) * (IDX_DIM ** -0.5)
    expand_heads(qi, qi_out)

    kiT_out[0] = lax.dot_general(wkiT_ref[...], h, nt, preferred_element_type=F32).astype(BF16)
    w_out[...] = jnp.dot(h, wwi_ref[...], preferred_element_type=F32) * (N_IDX_HEADS ** -0.5)


def _dsa_kernel(q_ref, qi_ref, w_ref, kT_ref, kiT_ref, v_ref, o_ref,
                s_ref, acc_ref, m_ref, l_ref, *, topk, kc):
    tq = q_ref.shape[1]
    seq = s_ref.shape[1]
    n = pl.program_id(1)
    t0 = n * tq
    nchunks = (t0 + tq + kc - 1) // kc
    qpos = t0 + lax.broadcasted_iota(I32, (tq, 1), 0)
    kf = float(topk)

    def score_chunk(j, carry):
        off = pl.multiple_of(j * kc, kc)
        ki = kiT_ref[0, :, pl.ds(off, kc)]
        acc = jnp.zeros((tq, kc), F32)
        for hh in range(N_IDX_HEADS):
            z = jnp.dot(qi_ref[0, :, hh * LANES:(hh + 1) * LANES], ki, preferred_element_type=F32)
            acc = acc + jnp.maximum(z, 0.0) * w_ref[0, :, hh:hh + 1]
        kpos = off + lax.broadcasted_iota(I32, (tq, kc), 1)
        s_ref[:, pl.ds(off, kc)] = jnp.where(kpos <= qpos, acc, -jnp.inf)
        return carry

    lax.fori_loop(0, nchunks, score_chunk, 0)

    def count_rows(pred):
        def body(j, acc):
            off = pl.multiple_of(j * kc, kc)
            for u in range(kc // LANES):
                s = s_ref[:, pl.ds(off + u * LANES, LANES)]
                acc = acc + jnp.where(pred(s, off + u * LANES), 1.0, 0.0)
            return acc
        acc = lax.fori_loop(0, nchunks, body, jnp.zeros((tq, LANES), F32))
        return jnp.sum(acc, axis=-1, keepdims=True)

    def count_ge(t_col):
        tb = jnp.broadcast_to(t_col, (tq, LANES))
        return count_rows(lambda s, _: s >= tb)

    def active_rows(lo, hi, cnt_lo):
        return (cnt_lo > kf) & ((hi - lo) != 1)

    def n_true(mask):
        return jnp.sum(jnp.where(mask, 1.0, 0.0))

    def bis_cond(st):
        return st[0] > 0.0

    def bis_body(st):
        _, lo, hi, cnt_lo, cnt_hi = st
        act = active_rows(lo, hi, cnt_lo)
        mid = lo + lax.shift_right_logical(hi - lo, jnp.ones_like(lo))
        c = count_ge(_key_to_f32(mid))
        ge = c >= kf
        take_lo = act & ge
        take_hi = act & jnp.logical_not(ge)
        lo = jnp.where(take_lo, mid, lo)
        cnt_lo = jnp.where(take_lo, c, cnt_lo)
        hi = jnp.where(take_hi, mid, hi)
        cnt_hi = jnp.where(take_hi, c, cnt_hi)
        return (n_true(active_rows(lo, hi, cnt_lo)), lo, hi, cnt_lo, cnt_hi)

    lo0 = jnp.full((tq, 1), KEY_LOWEST, I32)
    hi0 = jnp.full((tq, 1), KEY_INF, I32)
    cnt_lo0 = (qpos + 1).astype(F32)
    cnt_hi0 = jnp.zeros((tq, 1), F32)
    st0 = (n_true(active_rows(lo0, hi0, cnt_lo0)), lo0, hi0, cnt_lo0, cnt_hi0)
    _, lo, hi, cnt_lo, cnt_hi = lax.while_loop(bis_cond, bis_body, st0)

    t_lo = _key_to_f32(lo)
    tie = cnt_lo > kf
    t_hi = jnp.where(tie, _key_to_f32(hi), t_lo)
    need = kf - cnt_hi

    def tie_cut():
        tlb = jnp.broadcast_to(t_lo, (tq, LANES))
        thb = jnp.broadcast_to(t_hi, (tq, LANES))
        lane = lax.broadcasted_iota(I32, (tq, LANES), 1)

        def step(_, st):
            jl, jh = st
            jm = jl + ((jh - jl) >> 1)
            jmb = jnp.broadcast_to(jm, (tq, LANES))
            c = count_rows(lambda s, base: (s >= tlb) & (s < thb) & (lane <= jmb - base))
            ok = c >= need
            return jnp.where(ok, jl, jm), jnp.where(ok, jm, jh)

        jl0 = jnp.full((tq, 1), -1, I32)
        jh0 = jnp.full((tq, 1), seq - 1, I32)
        _, jh = lax.fori_loop(0, max(1, math.ceil(math.log2(seq + 1))), step, (jl0, jh0))
        return jnp.where(tie, jh, seq)

    jcut = lax.cond(n_true(tie) > 0.0, tie_cut, lambda: jnp.full((tq, 1), seq, I32))

    m_ref[...] = jnp.full(m_ref.shape, MASK_VAL, F32)
    l_ref[...] = jnp.zeros(l_ref.shape, F32)
    acc_ref[...] = jnp.zeros(acc_ref.shape, F32)
    lane128 = lax.broadcasted_iota(I32, (tq, LANES), 1)
    lo_half = lane128 < HEAD_DIM

    def attn_chunk(j, carry):
        off = pl.multiple_of(j * kc, kc)
        s = s_ref[:, pl.ds(off, kc)]
        kidx = off + lax.broadcasted_iota(I32, (tq, kc), 1)
        sel = (s >= t_hi) | ((s >= t_lo) & (kidx <= jcut))
        bias = jnp.where(sel, 0.0, MASK_VAL)
        for pair in range(N_HEADS // 2):
            kt = kT_ref[0, pair * LANES:(pair + 1) * LANES, pl.ds(off, kc)]
            vp = v_ref[0, pl.ds(off, kc), pair * LANES:(pair + 1) * LANES]
            outs, alphas = [], []
            for e in range(2):
                hh = 2 * pair + e
                qh = q_ref[0, :, hh * LANES:(hh + 1) * LANES]
                lg = jnp.dot(qh, kt, preferred_element_type=F32) + bias
                m_old = m_ref[hh]
                m_new = jnp.maximum(m_old, jnp.max(lg, axis=-1, keepdims=True))
                p = jnp.exp2(lg - m_new)
                alpha = jnp.exp2(m_old - m_new)
                l_ref[hh] = alpha * l_ref[hh] + jnp.sum(p, axis=-1, keepdims=True)
                m_ref[hh] = m_new
                outs.append(jnp.dot(p.astype(BF16), vp, preferred_element_type=F32))
                alphas.append(alpha)
            new = jnp.where(lo_half, outs[0], outs[1])
            a = jnp.where(lo_half, alphas[0], alphas[1])
            sl = slice(pair * LANES, (pair + 1) * LANES)
            acc_ref[:, sl] = a * acc_ref[:, sl] + new
        return carry

    lax.fori_loop(0, nchunks, attn_chunk, 0)

    for pair in range(N_HEADS // 2):
        sl = slice(pair * LANES, (pair + 1) * LANES)
        lsum = jnp.where(lo_half, l_ref[2 * pair], l_ref[2 * pair + 1])
        o_ref[0, :, sl] = (acc_ref[:, sl] / lsum).astype(o_ref.dtype)


def _conv_kernel(x_ref, gmix_ref, wci_ref, cw_ref, cb_ref, lng_ref, lnb_ref, o_ref, ext_ref):
    tm = x_ref.shape[1]
    i = pl.program_id(1)
    h = _rms_h(x_ref[0], gmix_ref[...])
    u = jnp.dot(h, wci_ref[...], preferred_element_type=F32)
    glu = u[:, :CONV_CH] * jax.nn.sigmoid(u[:, CONV_CH:])

    @pl.when(i == 0)
    def _():
        ext_ref[0:CONV_HALO, :] = jnp.zeros((CONV_HALO, CONV_CH), F32)

    ext_ref[CONV_HALO:CONV_HALO + tm, :] = glu
    y = jnp.broadcast_to(cb_ref[...], (tm, CONV_CH))
    base = CONV_HALO - (CONV_KERNEL - 1)
    for j in range(CONV_KERNEL):
        y = y + cw_ref[j:j + 1, :] * ext_ref[base + j:base + j + tm, :]
    ext_ref[0:CONV_HALO, :] = ext_ref[tm:tm + CONV_HALO, :]

    mu = jnp.mean(y, axis=-1, keepdims=True)
    yc = y - mu
    var = jnp.mean(yc * yc, axis=-1, keepdims=True)
    z = yc * lax.rsqrt(var + EPS) * lng_ref[...] + lnb_ref[...]
    o_ref[0] = (z * jax.nn.sigmoid(z)).astype(o_ref.dtype)


def _mix_kernel(x_ref, attn_ref, conv_ref, p_ref, gmix_ref, wga_ref, wgc_ref, wao_ref, wco_ref, wout_ref,
                gffn_ref, wfg_ref, wfu_ref, wfd_ref, gple_ref, wpg_ref, wpp_ref, o_ref, *, ff_chunk):
    x = x_ref[...]
    h = _rms_h(x, gmix_ref[...])
    ga = jax.nn.sigmoid(jnp.dot(h, wga_ref[...], preferred_element_type=F32))
    gc = jax.nn.sigmoid(jnp.dot(h, wgc_ref[...], preferred_element_type=F32))
    merged = (ga * jnp.dot(attn_ref[...], wao_ref[...], preferred_element_type=F32)
              + gc * jnp.dot(conv_ref[...], wco_ref[...], preferred_element_type=F32))
    x = x + jnp.dot(merged.astype(BF16), wout_ref[...], preferred_element_type=F32)

    hf = _rms_h(x, gffn_ref[...])
    d_ff = wfg_ref.shape[1]
    ffn = jnp.zeros(x.shape, F32)
    for c in range(d_ff // ff_chunk):
        sl = slice(c * ff_chunk, (c + 1) * ff_chunk)
        g = jnp.dot(hf, wfg_ref[:, sl], preferred_element_type=F32)
        up = jnp.dot(hf, wfu_ref[:, sl], preferred_element_type=F32)
        act = (g * jax.nn.sigmoid(g) * up).astype(BF16)
        ffn = ffn + jnp.dot(act, wfd_ref[sl, :], preferred_element_type=F32)
    x = x + ffn

    hp = _rms_h(x, gple_ref[...])
    gate = jax.nn.sigmoid(jnp.dot(hp, wpg_ref[...], preferred_element_type=F32))
    emb = jnp.dot(p_ref[...].astype(BF16), wpp_ref[...], preferred_element_type=F32)
    o_ref[...] = x + gate * emb


def _const_spec(shape):
    nd = len(shape)
    return pl.BlockSpec(shape, lambda *_: (0,) * nd, pipeline_mode=pl.Buffered(1))


def _layer(x, p, g_mix, w_in, g_q, g_k, conv_w, conv_b, conv_ln_g, conv_ln_b, w_attn_o, w_conv_o, w_out,
           g_ffn, w_ffn_gate, w_ffn_up, w_ffn_down, g_ple, w_ple_gate, w_ple_proj):
    B, S, D = x.shape
    T = B * S
    topk = min(TOPK_MAX, S // 4)
    x2 = x.reshape(T, D)

    sizes = (ATTN_WIDTH, ATTN_WIDTH, ATTN_WIDTH, N_IDX_HEADS * IDX_DIM, IDX_DIM, N_IDX_HEADS,
             2 * CONV_CH, D, D)
    offs = [0]
    for s in sizes:
        offs.append(offs[-1] + s)
    wcols = [w_in[:, offs[i]:offs[i + 1]] for i in range(len(sizes))]
    wq, wk, wv, wqi, wki, wwi, wci, wga, wgc = wcols
    wq = wq.astype(BF16)
    wkT = wk.T.astype(BF16)
    wv = wv.astype(BF16)
    wqi = wqi.astype(BF16)
    wkiT2 = jnp.concatenate([wki.T, wki.T], axis=0).astype(BF16)
    wwi_p = jnp.pad(wwi, ((0, 0), (0, LANES - N_IDX_HEADS))).astype(BF16)
    gmix = g_mix.reshape(1, D)
    q_scale = (HEAD_DIM ** -0.5) * math.log2(math.e)
    gq = (jnp.tile(g_q, N_HEADS) * q_scale).reshape(1, ATTN_WIDTH)
    gk = jnp.tile(g_k, N_HEADS).reshape(ATTN_WIDTH, 1)
    head_id = jnp.arange(ATTN_WIDTH) // HEAD_DIM
    hd = (head_id[:, None] == head_id[None, :]).astype(BF16)

    tm = 512
    ns = S // tm
    proj = pl.pallas_call(
        _proj_kernel,
        grid=(T // tm,),
        in_specs=[
            pl.BlockSpec((tm, D), lambda i: (i, 0)),
            _const_spec((1, D)),
            _const_spec((D, ATTN_WIDTH)), _const_spec((ATTN_WIDTH, D)), _const_spec((D, ATTN_WIDTH)),
            _const_spec((D, ATTN_WIDTH)), _const_spec((LANES, D)), _const_spec((D, LANES)),
            _const_spec((1, ATTN_WIDTH)), _const_spec((ATTN_WIDTH, 1)), _const_spec((ATTN_WIDTH, ATTN_WIDTH)),
        ],
        out_specs=[
            pl.BlockSpec((tm, 2 * ATTN_WIDTH), lambda i: (i, 0)),
            pl.BlockSpec((1, ATTN_WIDTH, tm), lambda i: (i // ns, 0, i % ns)),
            pl.BlockSpec((tm, ATTN_WIDTH), lambda i: (i, 0)),
            pl.BlockSpec((tm, 2 * ATTN_WIDTH), lambda i: (i, 0)),
            pl.BlockSpec((1, LANES, tm), lambda i: (i // ns, 0, i % ns)),
            pl.BlockSpec((tm, LANES), lambda i: (i, 0)),
        ],
        out_shape=[
            jax.ShapeDtypeStruct((T, 2 * ATTN_WIDTH), BF16),
            jax.ShapeDtypeStruct((B, ATTN_WIDTH, S), BF16),
            jax.ShapeDtypeStruct((T, ATTN_WIDTH), BF16),
            jax.ShapeDtypeStruct((T, 2 * ATTN_WIDTH), BF16),
            jax.ShapeDtypeStruct((B, LANES, S), BF16),
            jax.ShapeDtypeStruct((T, LANES), F32),
        ],
        compiler_params=pltpu.CompilerParams(dimension_semantics=("arbitrary",), vmem_limit_bytes=VMEM_LIMIT),
        name="dsa_proj",
    )
    q_e, kT, v, qi_e, kiT2, wsc = proj(x2, gmix, wq, wkT, wv, wqi, wkiT2, wwi_p, gq, gk, hd)

    tq = 128
    kc = 256
    nq = S // tq
    dsa = pl.pallas_call(
        functools.partial(_dsa_kernel, topk=topk, kc=kc),
        grid=(B, nq),
        in_specs=[
            pl.BlockSpec((1, tq, 2 * ATTN_WIDTH), lambda b, n: (b, n, 0)),
            pl.BlockSpec((1, tq, 2 * ATTN_WIDTH), lambda b, n: (b, n, 0)),
            pl.BlockSpec((1, tq, LANES), lambda b, n: (b, n, 0)),
            pl.BlockSpec((1, ATTN_WIDTH, S), lambda b, n: (b, 0, 0), pipeline_mode=pl.Buffered(1)),
            pl.BlockSpec((1, LANES, S), lambda b, n: (b, 0, 0), pipeline_mode=pl.Buffered(1)),
            pl.BlockSpec((1, S, ATTN_WIDTH), lambda b, n: (b, 0, 0), pipeline_mode=pl.Buffered(1)),
        ],
        out_specs=pl.BlockSpec((1, tq, ATTN_WIDTH), lambda b, n: (b, n, 0)),
        out_shape=jax.ShapeDtypeStruct((B, S, ATTN_WIDTH), BF16),
        scratch_shapes=[
            pltpu.VMEM((tq, S), F32),
            pltpu.VMEM((tq, ATTN_WIDTH), F32),
            pltpu.VMEM((N_HEADS, tq, 1), F32),
            pltpu.VMEM((N_HEADS, tq, 1), F32),
        ],
        compiler_params=pltpu.CompilerParams(dimension_semantics=("arbitrary", "arbitrary"),
                                             vmem_limit_bytes=VMEM_LIMIT),
        name="dsa_attn",
    )
    attn = dsa(q_e.reshape(B, S, 2 * ATTN_WIDTH), qi_e.reshape(B, S, 2 * ATTN_WIDTH),
               wsc.reshape(B, S, LANES), kT, kiT2, v.reshape(B, S, ATTN_WIDTH))

    tmc = 512
    cw = jnp.pad(conv_w.reshape(CONV_KERNEL, CONV_CH), ((0, CONV_HALO - CONV_KERNEL), (0, 0)))
    conv = pl.pallas_call(
        _conv_kernel,
        grid=(B, S // tmc),
        in_specs=[
            pl.BlockSpec((1, tmc, D), lambda b, i: (b, i, 0)),
            _const_spec((1, D)),
            _const_spec((D, 2 * CONV_CH)),
            _const_spec((CONV_HALO, CONV_CH)),
            _const_spec((1, CONV_CH)), _const_spec((1, CONV_CH)), _const_spec((1, CONV_CH)),
        ],
        out_specs=pl.BlockSpec((1, tmc, CONV_CH), lambda b, i: (b, i, 0)),
        out_shape=jax.ShapeDtypeStruct((B, S, CONV_CH), BF16),
        scratch_shapes=[pltpu.VMEM((tmc + CONV_HALO, CONV_CH), F32)],
        compiler_params=pltpu.CompilerParams(dimension_semantics=("arbitrary", "arbitrary"),
                                             vmem_limit_bytes=VMEM_LIMIT),
        name="conv_module",
    )(x, gmix, wci.astype(BF16), cw, conv_b.reshape(1, CONV_CH), conv_ln_g.reshape(1, CONV_CH),
      conv_ln_b.reshape(1, CONV_CH))

    tmd = 256
    d_ff = w_ffn_gate.shape[1]
    ple = p.shape[-1]
    ff_chunk = 256 if d_ff % 256 == 0 else d_ff
    out = pl.pallas_call(
        functools.partial(_mix_kernel, ff_chunk=ff_chunk),
        grid=(T // tmd,),
        in_specs=[
            pl.BlockSpec((tmd, D), lambda i: (i, 0)),
            pl.BlockSpec((tmd, ATTN_WIDTH), lambda i: (i, 0)),
            pl.BlockSpec((tmd, CONV_CH), lambda i: (i, 0)),
            pl.BlockSpec((tmd, ple), lambda i: (i, 0)),
            _const_spec((1, D)),
            _const_spec((D, D)), _const_spec((D, D)),
            _const_spec((ATTN_WIDTH, D)), _const_spec((CONV_CH, D)), _const_spec((D, D)),
            _const_spec((1, D)),
            _const_spec((D, d_ff)), _const_spec((D, d_ff)), _const_spec((d_ff, D)),
            _const_spec((1, D)),
            _const_spec((D, D)), _const_spec((ple, D)),
        ],
        out_specs=pl.BlockSpec((tmd, D), lambda i: (i, 0)),
        out_shape=jax.ShapeDtypeStruct((T, D), F32),
        compiler_params=pltpu.CompilerParams(dimension_semantics=("arbitrary",), vmem_limit_bytes=VMEM_LIMIT),
        name="mix_ffn_ple",
    )(x2, attn.reshape(T, ATTN_WIDTH), conv.reshape(T, CONV_CH), p.reshape(T, ple),
      gmix, wga.astype(BF16), wgc.astype(BF16), w_attn_o.astype(BF16), w_conv_o.astype(BF16),
      w_out.astype(BF16), g_ffn.reshape(1, D), w_ffn_gate.astype(BF16), w_ffn_up.astype(BF16),
      w_ffn_down.astype(BF16), g_ple.reshape(1, D), w_ple_gate.astype(BF16), w_ple_proj.astype(BF16))
    return out.reshape(B, S, D)


def kernel(x, p, g_mix, w_in, g_q, g_k, conv_w, conv_b, conv_ln_g, conv_ln_b, w_attn_o, w_conv_o, w_out,
           g_ffn, w_ffn_gate, w_ffn_up, w_ffn_down, g_ple, w_ple_gate, w_ple_proj):
    depth = w_in.shape[0]
    for i in range(depth):
        x = _layer(x, p[i], g_mix[i], w_in[i], g_q[i], g_k[i], conv_w[i], conv_b[i], conv_ln_g[i],
                   conv_ln_b[i], w_attn_o[i], w_conv_o[i], w_out[i], g_ffn[i], w_ffn_gate[i],
                   w_ffn_up[i], w_ffn_down[i], g_ple[i], w_ple_gate[i], w_ple_proj[i])
    return x
```

```python
import functools
import math

import jax
import jax.numpy as jnp
from jax import lax
from jax.experimental import pallas as pl
from jax.experimental.pallas import tpu as pltpu

F32 = jnp.float32
BF16 = jnp.bfloat16
I32 = jnp.int32

N_HEADS = 8
HEAD_DIM = 64
ATTN_WIDTH = N_HEADS * HEAD_DIM
N_IDX_HEADS = 8
IDX_DIM = 64
TOPK_MAX = 256
CONV_CH = 512
CONV_KERNEL = 31
EPS = 1e-6

LANES = 128
SUBLANES = 8
CONV_HALO = 32
VMEM_LIMIT = 56 * 1024 * 1024

KEY_LOWEST = -2139095040
KEY_INF = 2139095040
MASK_VAL = -1e30

NT_DIMS = (((1,), (1,)), ((), ()))


def _key_to_f32(k):
    bits = k ^ ((k >> 31) & 0x7FFFFFFF)
    return lax.bitcast_convert_type(bits, F32)


def _rms_h(x, g):
    ms = jnp.mean(x * x, axis=-1, keepdims=True)
    return (x * lax.rsqrt(ms + EPS) * g).astype(BF16)


def _split_bf16(a):
    hi = a.astype(BF16)
    lo = (a - hi.astype(F32)).astype(BF16)
    return hi, lo


def _proj_kernel(x_ref, gmix_ref, wqT_ref, wk_ref, wvT_ref, wqiT_ref, wki_ref, wwiT_ref,
                 gq_ref, gk_ref, hd_ref,
                 qT_out, k_out, vT_out, qiT_out, ki_out, wT_out):
    tm = x_ref.shape[0]
    h = _rms_h(x_ref[...], gmix_ref[...])
    hd = hd_ref[...]

    row = lax.broadcasted_iota(I32, (LANES, tm), 0)
    top_half = row < HEAD_DIM

    def expand_heads(yT, out_ref):
        for pair in range(N_HEADS // 2):
            blk = yT[pair * LANES:(pair + 1) * LANES, :]
            out_ref[0, (2 * pair) * LANES:(2 * pair + 1) * LANES, :] = jnp.where(top_half, blk, 0.0).astype(BF16)
            out_ref[0, (2 * pair + 1) * LANES:(2 * pair + 2) * LANES, :] = jnp.where(top_half, 0.0, blk).astype(BF16)

    qT = lax.dot_general(wqT_ref[...], h, NT_DIMS, preferred_element_type=F32)
    hi, lo = _split_bf16(qT * qT)
    ssq = jnp.dot(hd, hi, preferred_element_type=F32) + jnp.dot(hd, lo, preferred_element_type=F32)
    expand_heads(qT * lax.rsqrt(ssq * (1.0 / HEAD_DIM) + EPS) * gq_ref[...], qT_out)

    k = jnp.dot(h, wk_ref[...], preferred_element_type=F32)
    hi, lo = _split_bf16(k * k)
    ssk = jnp.dot(hi, hd, preferred_element_type=F32) + jnp.dot(lo, hd, preferred_element_type=F32)
    k_out[...] = (k * lax.rsqrt(ssk * (1.0 / HEAD_DIM) + EPS) * gk_ref[...]).astype(BF16)

    vT_out[0] = lax.dot_general(wvT_ref[...], h, NT_DIMS, preferred_element_type=F32).astype(BF16)

    qiT = lax.dot_general(wqiT_ref[...], h, NT_DIMS, preferred_element_type=F32) * (IDX_DIM ** -0.5)
    expand_heads(qiT, qiT_out)

    ki_out[...] = jnp.dot(h, wki_ref[...], preferred_element_type=F32).astype(BF16)
    wT_out[0] = (lax.dot_general(wwiT_ref[...], h, NT_DIMS, preferred_element_type=F32)
                 * (N_IDX_HEADS ** -0.5))


def _dsa_kernel(qT_ref, qiT_ref, wT_ref, k_ref, ki_ref, vT_ref, o_ref,
                sT_ref, accT_ref, lg_ref, *, topk, kc_score, kc_count, kc_attn):
    tq = qT_ref.shape[2]
    seq = sT_ref.shape[0]
    n = pl.program_id(1)
    t0 = n * tq
    n_keys = t0 + tq
    qpos = t0 + lax.broadcasted_iota(I32, (1, tq), 1)
    kf = float(topk)

    def score_chunk(j, masked):
        off = pl.multiple_of(j * kc_score, kc_score)
        ki = ki_ref[0, pl.ds(off, kc_score), :]
        acc = jnp.zeros((kc_score, tq), F32)
        for hh in range(N_IDX_HEADS):
            z = jnp.dot(ki, qiT_ref[0, hh * LANES:(hh + 1) * LANES, :], preferred_element_type=F32)
            acc = acc + jnp.maximum(z, 0.0) * wT_ref[0, hh:hh + 1, :]
        if masked:
            kpos = off + lax.broadcasted_iota(I32, (kc_score, tq), 0)
            acc = jnp.where(kpos <= qpos, acc, -jnp.inf)
        sT_ref[pl.ds(off, kc_score), :] = acc

    n_full = t0 // kc_score

    def full_chunk(j, carry):
        score_chunk(j, False)
        return carry

    lax.fori_loop(0, n_full, full_chunk, 0)
    for d in range(tq // kc_score):
        score_chunk(n_full + d, True)

    n_cnt = n_keys // kc_count

    def count_keys(pred):
        def body(j, acc):
            off = pl.multiple_of(j * kc_count, kc_count)
            ind = jnp.where(pred(sT_ref[pl.ds(off, kc_count), :], off), 1.0, 0.0)
            return acc + jnp.sum(ind.reshape(kc_count // SUBLANES, SUBLANES, tq), axis=0)
        acc = lax.fori_loop(0, n_cnt, body, jnp.zeros((SUBLANES, tq), F32))
        return jnp.sum(acc, axis=0, keepdims=True)

    def count_ge(t_row):
        return count_keys(lambda s, _: s >= t_row)

    def active_queries(lo, hi, cnt_lo):
        return (cnt_lo > kf) & ((hi - lo) != 1)

    def n_true(mask):
        return jnp.sum(jnp.where(mask, 1.0, 0.0))

    def bis_cond(st):
        return st[0] > 0.0

    def bis_body(st):
        _, lo, hi, cnt_lo, cnt_hi = st
        act = active_queries(lo, hi, cnt_lo)
        mid = lo + lax.shift_right_logical(hi - lo, jnp.ones_like(lo))
        c = count_ge(_key_to_f32(mid))
        ge = c >= kf
        take_lo = act & ge
        take_hi = act & jnp.logical_not(ge)
        lo = jnp.where(take_lo, mid, lo)
        cnt_lo = jnp.where(take_lo, c, cnt_lo)
        hi = jnp.where(take_hi, mid, hi)
        cnt_hi = jnp.where(take_hi, c, cnt_hi)
        return (n_true(active_queries(lo, hi, cnt_lo)), lo, hi, cnt_lo, cnt_hi)

    lo0 = jnp.full((1, tq), KEY_LOWEST, I32)
    hi0 = jnp.full((1, tq), KEY_INF, I32)
    cnt_lo0 = (qpos + 1).astype(F32)
    cnt_hi0 = jnp.zeros((1, tq), F32)
    st0 = (n_true(active_queries(lo0, hi0, cnt_lo0)), lo0, hi0, cnt_lo0, cnt_hi0)
    _, lo, hi, cnt_lo, cnt_hi = lax.while_loop(bis_cond, bis_body, st0)

    t_lo = _key_to_f32(lo)
    tie = cnt_lo > kf
    t_hi = jnp.where(tie, _key_to_f32(hi), t_lo)
    need = kf - cnt_hi

    def tie_cut():
        krow = lax.broadcasted_iota(I32, (kc_count, tq), 0)

        def step(_, st):
            jl, jh = st
            jm = jl + ((jh - jl) >> 1)
            c = count_keys(lambda s, base: (s >= t_lo) & (s < t_hi) & (krow <= jm - base))
            ok = c >= need
            return jnp.where(ok, jl, jm), jnp.where(ok, jm, jh)

        jl0 = jnp.full((1, tq), -1, I32)
        jh0 = jnp.full((1, tq), seq - 1, I32)
        _, jh = lax.fori_loop(0, max(1, math.ceil(math.log2(seq + 1))), step, (jl0, jh0))
        return jnp.where(tie, jh, seq)

    jcut = lax.cond(n_true(tie) > 0.0, tie_cut, lambda: jnp.full((1, tq), seq, I32))

    accT_ref[...] = jnp.zeros(accT_ref.shape, F32)

    def attn_chunk(j, carry):
        m_old, l_old = carry
        off = pl.multiple_of(j * kc_attn, kc_attn)
        s = sT_ref[pl.ds(off, kc_attn), :]
        kidx = off + lax.broadcasted_iota(I32, (kc_attn, tq), 0)
        sel = (s >= t_hi) | ((s >= t_lo) & (kidx <= jcut))
        bias = jnp.where(sel, 0.0, MASK_VAL)
        cmax = []
        for hh in range(N_HEADS):
            pair = hh // 2
            kp = k_ref[0, pl.ds(off, kc_attn), pair * LANES:(pair + 1) * LANES]
            lg = jnp.dot(kp, qT_ref[0, hh * LANES:(hh + 1) * LANES, :], preferred_element_type=F32) + bias
            lg_ref[hh] = lg
            cmax.append(jnp.max(lg, axis=0, keepdims=True))
        m_new = jnp.maximum(m_old, jnp.concatenate(cmax, axis=0))
        alpha = jnp.exp2(m_old - m_new)
        psum = []
        for hh in range(N_HEADS):
            p = jnp.exp2(lg_ref[hh] - m_new[hh:hh + 1, :])
            psum.append(jnp.sum(p, axis=0, keepdims=True))
            hs = slice(hh * HEAD_DIM, (hh + 1) * HEAD_DIM)
            pv = jnp.dot(vT_ref[0, hs, pl.ds(off, kc_attn)], p.astype(BF16), preferred_element_type=F32)
            accT_ref[hs, :] = alpha[hh:hh + 1, :] * accT_ref[hs, :] + pv
        return m_new, alpha * l_old + jnp.concatenate(psum, axis=0)

    m0 = jnp.full((N_HEADS, tq), MASK_VAL, F32)
    l0 = jnp.zeros((N_HEADS, tq), F32)
    _, lsum = lax.fori_loop(0, n_keys // kc_attn, attn_chunk, (m0, l0))

    for hh in range(N_HEADS):
        hs = slice(hh * HEAD_DIM, (hh + 1) * HEAD_DIM)
        accT_ref[hs, :] = accT_ref[hs, :] / lsum[hh:hh + 1, :]
    o_ref[0] = accT_ref[...].T.astype(o_ref.dtype)


def _conv_kernel(x_ref, gmix_ref, wci_ref, cw_ref, cb_ref, lng_ref, lnb_ref, o_ref, ext_ref):
    tm = x_ref.shape[1]
    i = pl.program_id(1)
    h = _rms_h(x_ref[0], gmix_ref[...])
    u = jnp.dot(h, wci_ref[...], preferred_element_type=F32)
    glu = u[:, :CONV_CH] * jax.nn.sigmoid(u[:, CONV_CH:])

    @pl.when(i == 0)
    def _():
        ext_ref[0:CONV_HALO, :] = jnp.zeros((CONV_HALO, CONV_CH), F32)

    ext_ref[CONV_HALO:CONV_HALO + tm, :] = glu
    y = jnp.broadcast_to(cb_ref[...], (tm, CONV_CH))
    base = CONV_HALO - (CONV_KERNEL - 1)
    for j in range(CONV_KERNEL):
        y = y + cw_ref[j:j + 1, :] * ext_ref[base + j:base + j + tm, :]
    ext_ref[0:CONV_HALO, :] = ext_ref[tm:tm + CONV_HALO, :]

    mu = jnp.mean(y, axis=-1, keepdims=True)
    yc = y - mu
    var = jnp.mean(yc * yc, axis=-1, keepdims=True)
    z = yc * lax.rsqrt(var + EPS) * lng_ref[...] + lnb_ref[...]
    o_ref[0] = (z * jax.nn.sigmoid(z)).astype(o_ref.dtype)


def _mix_kernel(x_ref, attn_ref, conv_ref, p_ref, gmix_ref, wga_ref, wgc_ref, wao_ref, wco_ref, wout_ref,
                gffn_ref, wfg_ref, wfu_ref, wfd_ref, gple_ref, wpg_ref, wpp_ref, o_ref, *, ff_chunk):
    x = x_ref[...]
    h = _rms_h(x, gmix_ref[...])
    ga = jax.nn.sigmoid(jnp.dot(h, wga_ref[...], preferred_element_type=F32))
    gc = jax.nn.sigmoid(jnp.dot(h, wgc_ref[...], preferred_element_type=F32))
    merged = (ga * jnp.dot(attn_ref[...], wao_ref[...], preferred_element_type=F32)
              + gc * jnp.dot(conv_ref[...], wco_ref[...], preferred_element_type=F32))
    x = x + jnp.dot(merged.astype(BF16), wout_ref[...], preferred_element_type=F32)

    hf = _rms_h(x, gffn_ref[...])
    d_ff = wfg_ref.shape[1]
    ffn = jnp.zeros(x.shape, F32)
    for c in range(d_ff // ff_chunk):
        sl = slice(c * ff_chunk, (c + 1) * ff_chunk)
        g = jnp.dot(hf, wfg_ref[:, sl], preferred_element_type=F32)
        up = jnp.dot(hf, wfu_ref[:, sl], preferred_element_type=F32)
        act = (g * jax.nn.sigmoid(g) * up).astype(BF16)
        ffn = ffn + jnp.dot(act, wfd_ref[sl, :], preferred_element_type=F32)
    x = x + ffn

    hp = _rms_h(x, gple_ref[...])
    gate = jax.nn.sigmoid(jnp.dot(hp, wpg_ref[...], preferred_element_type=F32))
    emb = jnp.dot(p_ref[...].astype(BF16), wpp_ref[...], preferred_element_type=F32)
    o_ref[...] = x + gate * emb


def _const_spec(shape):
    nd = len(shape)
    return pl.BlockSpec(shape, lambda *_: (0,) * nd, pipeline_mode=pl.Buffered(1))


def _layer(x, p, g_mix, w_in, g_q, g_k, conv_w, conv_b, conv_ln_g, conv_ln_b, w_attn_o, w_conv_o, w_out,
           g_ffn, w_ffn_gate, w_ffn_up, w_ffn_down, g_ple, w_ple_gate, w_ple_proj):
    B, S, D = x.shape
    T = B * S
    topk = min(TOPK_MAX, S // 4)
    x2 = x.reshape(T, D)

    sizes = (ATTN_WIDTH, ATTN_WIDTH, ATTN_WIDTH, N_IDX_HEADS * IDX_DIM, IDX_DIM, N_IDX_HEADS,
             2 * CONV_CH, D, D)
    offs = [0]
    for s in sizes:
        offs.append(offs[-1] + s)
    wcols = [w_in[:, offs[i]:offs[i + 1]] for i in range(len(sizes))]
    wq, wk, wv, wqi, wki, wwi, wci, wga, wgc = wcols
    wqT = wq.T.astype(BF16)
    wk = wk.astype(BF16)
    wvT = wv.T.astype(BF16)
    wqiT = wqi.T.astype(BF16)
    wki2 = jnp.concatenate([wki, wki], axis=1).astype(BF16)
    wwiT = wwi.T.astype(BF16)
    gmix = g_mix.reshape(1, D)
    q_scale = (HEAD_DIM ** -0.5) * math.log2(math.e)
    gq = (jnp.tile(g_q, N_HEADS) * q_scale).reshape(ATTN_WIDTH, 1)
    gk = jnp.tile(g_k, N_HEADS).reshape(1, ATTN_WIDTH)
    head_id = jnp.arange(ATTN_WIDTH) // HEAD_DIM
    hd = (head_id[:, None] == head_id[None, :]).astype(BF16)

    tm = 512
    ns = S // tm
    row_tile = lambda i: (i, 0)
    seq_tile = lambda i: (i // ns, 0, i % ns)
    proj = pl.pallas_call(
        _proj_kernel,
        grid=(T // tm,),
        in_specs=[
            pl.BlockSpec((tm, D), row_tile),
            _const_spec((1, D)),
            _const_spec((ATTN_WIDTH, D)), _const_spec((D, ATTN_WIDTH)), _const_spec((ATTN_WIDTH, D)),
            _const_spec((ATTN_WIDTH, D)), _const_spec((D, LANES)), _const_spec((N_IDX_HEADS, D)),
            _const_spec((ATTN_WIDTH, 1)), _const_spec((1, ATTN_WIDTH)), _const_spec((ATTN_WIDTH, ATTN_WIDTH)),
        ],
        out_specs=[
            pl.BlockSpec((1, 2 * ATTN_WIDTH, tm), seq_tile),
            pl.BlockSpec((tm, ATTN_WIDTH), row_tile),
            pl.BlockSpec((1, ATTN_WIDTH, tm), seq_tile),
            pl.BlockSpec((1, 2 * ATTN_WIDTH, tm), seq_tile),
            pl.BlockSpec((tm, LANES), row_tile),
            pl.BlockSpec((1, N_IDX_HEADS, tm), seq_tile),
        ],
        out_shape=[
            jax.ShapeDtypeStruct((B, 2 * ATTN_WIDTH, S), BF16),
            jax.ShapeDtypeStruct((T, ATTN_WIDTH), BF16),
            jax.ShapeDtypeStruct((B, ATTN_WIDTH, S), BF16),
            jax.ShapeDtypeStruct((B, 2 * ATTN_WIDTH, S), BF16),
            jax.ShapeDtypeStruct((T, LANES), BF16),
            jax.ShapeDtypeStruct((B, N_IDX_HEADS, S), F32),
        ],
        compiler_params=pltpu.CompilerParams(dimension_semantics=("arbitrary",), vmem_limit_bytes=VMEM_LIMIT),
        name="dsa_proj",
    )
    qT, k, vT, qiT, ki2, wT = proj(x2, gmix, wqT, wk, wvT, wqiT, wki2, wwiT, gq, gk, hd)

    tq = 256
    kc_attn = 128
    resident = functools.partial(pl.BlockSpec, pipeline_mode=pl.Buffered(1))
    dsa = pl.pallas_call(
        functools.partial(_dsa_kernel, topk=topk, kc_score=128, kc_count=256, kc_attn=kc_attn),
        grid=(B, S // tq),
        in_specs=[
            pl.BlockSpec((1, 2 * ATTN_WIDTH, tq), lambda b, n: (b, 0, n)),
            pl.BlockSpec((1, 2 * ATTN_WIDTH, tq), lambda b, n: (b, 0, n)),
            pl.BlockSpec((1, N_IDX_HEADS, tq), lambda b, n: (b, 0, n)),
            resident((1, S, ATTN_WIDTH), lambda b, n: (b, 0, 0)),
            resident((1, S, LANES), lambda b, n: (b, 0, 0)),
            resident((1, ATTN_WIDTH, S), lambda b, n: (b, 0, 0)),
        ],
        out_specs=pl.BlockSpec((1, tq, ATTN_WIDTH), lambda b, n: (b, n, 0)),
        out_shape=jax.ShapeDtypeStruct((B, S, ATTN_WIDTH), BF16),
        scratch_shapes=[
            pltpu.VMEM((S, tq), F32),
            pltpu.VMEM((ATTN_WIDTH, tq), F32),
            pltpu.VMEM((N_HEADS, kc_attn, tq), F32),
        ],
        compiler_params=pltpu.CompilerParams(dimension_semantics=("arbitrary", "arbitrary"),
                                             vmem_limit_bytes=VMEM_LIMIT),
        name="dsa_attn",
    )
    attn = dsa(qT, qiT, wT, k.reshape(B, S, ATTN_WIDTH), ki2.reshape(B, S, LANES), vT)

    tmc = 512
    cw = jnp.pad(conv_w.reshape(CONV_KERNEL, CONV_CH), ((0, CONV_HALO - CONV_KERNEL), (0, 0)))
    conv = pl.pallas_call(
        _conv_kernel,
        grid=(B, S // tmc),
        in_specs=[
            pl.BlockSpec((1, tmc, D), lambda b, i: (b, i, 0)),
            _const_spec((1, D)),
            _const_spec((D, 2 * CONV_CH)),
            _const_spec((CONV_HALO, CONV_CH)),
            _const_spec((1, CONV_CH)), _const_spec((1, CONV_CH)), _const_spec((1, CONV_CH)),
        ],
        out_specs=pl.BlockSpec((1, tmc, CONV_CH), lambda b, i: (b, i, 0)),
        out_shape=jax.ShapeDtypeStruct((B, S, CONV_CH), BF16),
        scratch_shapes=[pltpu.VMEM((tmc + CONV_HALO, CONV_CH), F32)],
        compiler_params=pltpu.CompilerParams(dimension_semantics=("arbitrary", "arbitrary"),
                                             vmem_limit_bytes=VMEM_LIMIT),
        name="conv_module",
    )(x, gmix, wci.astype(BF16), cw, conv_b.reshape(1, CONV_CH), conv_ln_g.reshape(1, CONV_CH),
      conv_ln_b.reshape(1, CONV_CH))

    tmd = 256
    d_ff = w_ffn_gate.shape[1]
    ple = p.shape[-1]
    ff_chunk = 256 if d_ff % 256 == 0 else d_ff
    out = pl.pallas_call(
        functools.partial(_mix_kernel, ff_chunk=ff_chunk),
        grid=(T // tmd,),
        in_specs=[
            pl.BlockSpec((tmd, D), row_tile),
            pl.BlockSpec((tmd, ATTN_WIDTH), row_tile),
            pl.BlockSpec((tmd, CONV_CH), row_tile),
            pl.BlockSpec((tmd, ple), row_tile),
            _const_spec((1, D)),
            _const_spec((D, D)), _const_spec((D, D)),
            _const_spec((ATTN_WIDTH, D)), _const_spec((CONV_CH, D)), _const_spec((D, D)),
            _const_spec((1, D)),
            _const_spec((D, d_ff)), _const_spec((D, d_ff)), _const_spec((d_ff, D)),
            _const_spec((1, D)),
            _const_spec((D, D)), _const_spec((ple, D)),
        ],
        out_specs=pl.BlockSpec((tmd, D), row_tile),
        out_shape=jax.ShapeDtypeStruct((T, D), F32),
        compiler_params=pltpu.CompilerParams(dimension_semantics=("arbitrary",), vmem_limit_bytes=VMEM_LIMIT),
        name="mix_ffn_ple",
    )(x2, attn.reshape(T, ATTN_WIDTH), conv.reshape(T, CONV_CH), p.reshape(T, ple),
      gmix, wga.astype(BF16), wgc.astype(BF16), w_attn_o.astype(BF16), w_conv_o.astype(BF16),
      w_out.astype(BF16), g_ffn.reshape(1, D), w_ffn_gate.astype(BF16), w_ffn_up.astype(BF16),
      w_ffn_down.astype(BF16), g_ple.reshape(1, D), w_ple_gate.astype(BF16), w_ple_proj.astype(BF16))
    return out.reshape(B, S, D)


def kernel(x, p, g_mix, w_in, g_q, g_k, conv_w, conv_b, conv_ln_g, conv_ln_b, w_attn_o, w_conv_o, w_out,
           g_ffn, w_ffn_gate, w_ffn_up, w_ffn_down, g_ple, w_ple_gate, w_ple_proj):
    depth = w_in.shape[0]
    for i in range(depth):
        x = _layer(x, p[i], g_mix[i], w_in[i], g_q[i], g_k[i], conv_w[i], conv_b[i], conv_ln_g[i],
                   conv_ln_b[i], w_attn_o[i], w_conv_o[i], w_out[i], g_ffn[i], w_ffn_gate[i],
                   w_ffn_up[i], w_ffn_down[i], g_ple[i], w_ple_gate[i], w_ple_proj[i])
    return x
```

```python
import functools
import math

import jax
import jax.numpy as jnp
from jax import lax
from jax.experimental import pallas as pl
from jax.experimental.pallas import tpu as pltpu

F32 = jnp.float32
BF16 = jnp.bfloat16
I32 = jnp.int32

N_HEADS = 8
HEAD_DIM = 64
ATTN_WIDTH = N_HEADS * HEAD_DIM
N_IDX_HEADS = 8
IDX_DIM = 64
TOPK_MAX = 256
CONV_CH = 512
CONV_KERNEL = 31
EPS = 1e-6

LANES = 128
SUBLANES = 8
CONV_HALO = 32
VMEM_LIMIT = 56 * 1024 * 1024

KEY_LOWEST = -2139095040
KEY_INF = 2139095040
MASK_VAL = -1e30
BRACKET_SMALL = 4.0
STAGNANT_PASSES = 2.0
BLIND_PASSES = 8

NT_DIMS = (((1,), (1,)), ((), ()))


def _key_to_f32(k):
    bits = k ^ ((k >> 31) & 0x7FFFFFFF)
    return lax.bitcast_convert_type(bits, F32)


def _f32_to_key(x):
    bits = lax.bitcast_convert_type(x, I32)
    return bits ^ ((bits >> 31) & 0x7FFFFFFF)


def _rms_h(x, g):
    ms = jnp.mean(x * x, axis=-1, keepdims=True)
    return (x * lax.rsqrt(ms + EPS) * g).astype(BF16)


def _split_bf16(a):
    hi = a.astype(BF16)
    lo = (a - hi.astype(F32)).astype(BF16)
    return hi, lo


def _proj_kernel(x_ref, gmix_ref, wqT_ref, wk_ref, wvT_ref, wqiT_ref, wki_ref, wwiT_ref,
                 gq_ref, gk_ref, hd_ref,
                 qT_out, k_out, vT_out, qiT_out, ki_out, wT_out):
    tm = x_ref.shape[0]
    h = _rms_h(x_ref[...], gmix_ref[...])
    hd = hd_ref[...]

    row = lax.broadcasted_iota(I32, (LANES, tm), 0)
    top_half = row < HEAD_DIM

    def expand_heads(yT, out_ref):
        for pair in range(N_HEADS // 2):
            blk = yT[pair * LANES:(pair + 1) * LANES, :]
            out_ref[0, (2 * pair) * LANES:(2 * pair + 1) * LANES, :] = jnp.where(top_half, blk, 0.0).astype(BF16)
            out_ref[0, (2 * pair + 1) * LANES:(2 * pair + 2) * LANES, :] = jnp.where(top_half, 0.0, blk).astype(BF16)

    qT = lax.dot_general(wqT_ref[...], h, NT_DIMS, preferred_element_type=F32)
    hi, lo = _split_bf16(qT * qT)
    ssq = jnp.dot(hd, hi, preferred_element_type=F32) + jnp.dot(hd, lo, preferred_element_type=F32)
    expand_heads(qT * lax.rsqrt(ssq * (1.0 / HEAD_DIM) + EPS) * gq_ref[...], qT_out)

    k = jnp.dot(h, wk_ref[...], preferred_element_type=F32)
    hi, lo = _split_bf16(k * k)
    ssk = jnp.dot(hi, hd, preferred_element_type=F32) + jnp.dot(lo, hd, preferred_element_type=F32)
    k_out[...] = (k * lax.rsqrt(ssk * (1.0 / HEAD_DIM) + EPS) * gk_ref[...]).astype(BF16)

    vT_out[0] = lax.dot_general(wvT_ref[...], h, NT_DIMS, preferred_element_type=F32).astype(BF16)

    qiT = lax.dot_general(wqiT_ref[...], h, NT_DIMS, preferred_element_type=F32) * (IDX_DIM ** -0.5)
    expand_heads(qiT, qiT_out)

    ki_out[...] = jnp.dot(h, wki_ref[...], preferred_element_type=F32).astype(BF16)
    wT_out[0] = (lax.dot_general(wwiT_ref[...], h, NT_DIMS, preferred_element_type=F32)
                 * (N_IDX_HEADS ** -0.5))


def _dsa_kernel(qT_ref, qiT_ref, wT_ref, k_ref, ki_ref, vT_ref, o_ref,
                sT_ref, accT_ref, lg_ref, *, topk, kc_score, kc_count, kc_attn):
    tq = qT_ref.shape[2]
    n = pl.program_id(1)
    t0 = n * tq
    n_keys = t0 + tq
    qpos = t0 + lax.broadcasted_iota(I32, (1, tq), 1)
    kf = float(topk)

    def score_chunk(j, masked):
        off = pl.multiple_of(j * kc_score, kc_score)
        ki = ki_ref[0, pl.ds(off, kc_score), :]
        acc = jnp.zeros((kc_score, tq), F32)
        for hh in range(N_IDX_HEADS):
            z = jnp.dot(ki, qiT_ref[0, hh * LANES:(hh + 1) * LANES, :], preferred_element_type=F32)
            acc = acc + jnp.maximum(z, 0.0) * wT_ref[0, hh:hh + 1, :]
        if masked:
            kpos = off + lax.broadcasted_iota(I32, (kc_score, tq), 0)
            acc = jnp.where(kpos <= qpos, acc, -jnp.inf)
        sT_ref[pl.ds(off, kc_score), :] = acc
        return acc

    def chunk_pair(i, gm):
        g0, g1 = gm
        return (jnp.maximum(g0, score_chunk(2 * i, False)), jnp.maximum(g1, score_chunk(2 * i + 1, False)))

    n_pairs = t0 // (2 * kc_score)
    ninf = jnp.full((kc_score, tq), -jnp.inf, F32)
    g0, g1 = lax.fori_loop(0, n_pairs, chunk_pair, (ninf, ninf))
    for d in range(tq // (2 * kc_score)):
        g0 = jnp.maximum(g0, score_chunk(2 * (n_pairs + d), True))
        g1 = jnp.maximum(g1, score_chunk(2 * (n_pairs + d) + 1, True))
    grp_min = jnp.min(jnp.minimum(g0, g1), axis=0, keepdims=True)
    row_max = jnp.max(jnp.maximum(g0, g1), axis=0, keepdims=True)

    n_cnt = n_keys // kc_count

    def sweep(init, body_fn):
        def body(j, acc):
            off = pl.multiple_of(j * kc_count, kc_count)
            return body_fn(acc, sT_ref[pl.ds(off, kc_count), :])
        return lax.fori_loop(0, n_cnt, body, init)

    def fold(x, op):
        return op(x.reshape(kc_count // SUBLANES, SUBLANES, tq), axis=0)

    def count_ge(t_row):
        acc = sweep(jnp.zeros((SUBLANES, tq), F32),
                    lambda a, s: a + fold(jnp.where(s >= t_row, 1.0, 0.0), jnp.sum))
        return jnp.sum(acc, axis=0, keepdims=True)

    def next_above(t_row):
        def body(acc, s):
            c, mn = acc
            gt = s > t_row
            return (c + fold(jnp.where(gt, 1.0, 0.0), jnp.sum),
                    jnp.minimum(mn, fold(jnp.where(gt, s, jnp.inf), jnp.min)))
        c, mn = sweep((jnp.zeros((SUBLANES, tq), F32), jnp.full((SUBLANES, tq), jnp.inf, F32)), body)
        return jnp.sum(c, axis=0, keepdims=True), jnp.min(mn, axis=0, keepdims=True)

    def active_queries(lo, hi, cnt_lo):
        return (cnt_lo > kf) & ((hi - lo) != 1)

    def n_true(mask):
        return jnp.sum(jnp.where(mask, 1.0, 0.0))

    few = qpos < 2 * kc_score
    lo0 = jnp.where(few, KEY_LOWEST, _f32_to_key(grp_min))
    hi0 = _f32_to_key(row_max) + 1
    cnt_lo0 = jnp.where(few, (qpos + 1).astype(F32), count_ge(_key_to_f32(lo0)))
    cnt_hi0 = jnp.zeros((1, tq), F32)

    def a_step(st):
        lo, hi, cnt_lo, cnt_hi, same = st
        act = active_queries(lo, hi, cnt_lo)
        mid = lo + lax.shift_right_logical(hi - lo, jnp.ones_like(lo))
        c = count_ge(_key_to_f32(mid))
        ge = c >= kf
        take_lo = act & ge
        take_hi = act & jnp.logical_not(ge)
        new_lo = jnp.where(take_lo, c, cnt_lo)
        new_hi = jnp.where(take_hi, c, cnt_hi)
        same = jnp.where((new_lo - new_hi) == (cnt_lo - cnt_hi), same + 1.0, 0.0)
        return jnp.where(take_lo, mid, lo), jnp.where(take_hi, mid, hi), new_lo, new_hi, same

    def n_wide(st):
        lo, hi, cnt_lo, cnt_hi, same = st
        wide = (active_queries(lo, hi, cnt_lo) & ((cnt_lo - cnt_hi) > BRACKET_SMALL)
                & (same < STAGNANT_PASSES))
        return n_true(wide)

    def a_body(carry):
        st = a_step(carry[1])
        return n_wide(st), st

    st = (lo0, hi0, cnt_lo0, cnt_hi0, jnp.zeros((1, tq), F32))
    st = lax.fori_loop(0, BLIND_PASSES, lambda _, c: a_step(c), st)
    _, (lo, hi, cnt_lo, cnt_hi, _) = lax.while_loop(lambda c: c[0] > 0.0, a_body, (n_wide(st), st))

    def b_cond(st):
        return st[0] > 0.0

    def b_body(st):
        _, lo, hi, cnt_lo, cnt_hi = st
        act = active_queries(lo, hi, cnt_lo)
        c_gt, nxt = next_above(_key_to_f32(lo))
        nxt_key = _f32_to_key(nxt)
        done = act & (c_gt < kf)
        step = act & jnp.logical_not(done)
        hi = jnp.where(done, nxt_key, hi)
        cnt_hi = jnp.where(done, c_gt, cnt_hi)
        lo = jnp.where(step, nxt_key, lo)
        cnt_lo = jnp.where(step, c_gt, cnt_lo)
        still = step & (cnt_lo > kf)
        return (n_true(still), lo, hi, cnt_lo, cnt_hi)

    act0 = active_queries(lo, hi, cnt_lo)
    _, lo_b, hi_b, cnt_lo_b, cnt_hi_b = lax.while_loop(
        b_cond, b_body, (n_true(act0), lo, hi, cnt_lo, cnt_hi))

    t_lo = _key_to_f32(lo_b)
    tie = cnt_lo_b > kf
    t_sure = jnp.where(tie, _key_to_f32(hi_b), t_lo)
    t_tie = jnp.where(tie, t_lo, jnp.nan)
    need = kf - cnt_hi_b

    accT_ref[...] = jnp.zeros(accT_ref.shape, F32)
    tri = (lax.broadcasted_iota(I32, (kc_attn, kc_attn), 0)
           >= lax.broadcasted_iota(I32, (kc_attn, kc_attn), 1)).astype(BF16)

    n_att = n_keys // kc_attn

    def logits_stage(j, slot, ties_seen):
        off = pl.multiple_of(j * kc_attn, kc_attn)
        s = sT_ref[pl.ds(off, kc_attn), :]
        is_tie = s == t_tie
        rank = jnp.dot(tri, jnp.where(is_tie, 1.0, 0.0).astype(BF16), preferred_element_type=F32)
        sel = (s >= t_sure) | (is_tie & (rank + ties_seen <= need))
        bias = jnp.where(sel, 0.0, MASK_VAL)
        cmax = []
        for hh in range(N_HEADS):
            pair = hh // 2
            kp = k_ref[0, pl.ds(off, kc_attn), pair * LANES:(pair + 1) * LANES]
            lg = jnp.dot(kp, qT_ref[0, hh * LANES:(hh + 1) * LANES, :], preferred_element_type=F32) + bias
            lg_ref[slot, hh] = lg
            cmax.append(jnp.max(lg, axis=0, keepdims=True))
        return jnp.concatenate(cmax, axis=0), ties_seen + rank[kc_attn - 1:kc_attn, :]

    def attn_chunk(j, carry):
        m_old, l_old, cmax, ties_seen = carry
        off = pl.multiple_of(j * kc_attn, kc_attn)
        slot = j % 2
        m_new = jnp.maximum(m_old, cmax)
        alpha = jnp.exp2(m_old - m_new)
        psum = []
        for hh in range(N_HEADS):
            p = jnp.exp2(lg_ref[slot, hh] - m_new[hh:hh + 1, :])
            psum.append(jnp.sum(p, axis=0, keepdims=True))
            hs = slice(hh * HEAD_DIM, (hh + 1) * HEAD_DIM)
            pv = jnp.dot(vT_ref[0, hs, pl.ds(off, kc_attn)], p.astype(BF16), preferred_element_type=F32)
            accT_ref[hs, :] = alpha[hh:hh + 1, :] * accT_ref[hs, :] + pv
        cmax_next, ties_seen = logits_stage(jnp.minimum(j + 1, n_att - 1), 1 - slot, ties_seen)
        return m_new, alpha * l_old + jnp.concatenate(psum, axis=0), cmax_next, ties_seen

    m0 = jnp.full((N_HEADS, tq), MASK_VAL, F32)
    l0 = jnp.zeros((N_HEADS, tq), F32)
    cmax0, ties0 = logits_stage(0, 0, jnp.zeros((1, tq), F32))
    _, lsum, _, _ = lax.fori_loop(0, n_att, attn_chunk, (m0, l0, cmax0, ties0))

    for hh in range(N_HEADS):
        hs = slice(hh * HEAD_DIM, (hh + 1) * HEAD_DIM)
        accT_ref[hs, :] = accT_ref[hs, :] / lsum[hh:hh + 1, :]
    o_ref[0] = accT_ref[...].T.astype(o_ref.dtype)


def _conv_kernel(x_ref, gmix_ref, wci_ref, cw_ref, cb_ref, lng_ref, lnb_ref, o_ref, ext_ref):
    tm = x_ref.shape[1]
    i = pl.program_id(1)
    h = _rms_h(x_ref[0], gmix_ref[...])
    u = jnp.dot(h, wci_ref[...], preferred_element_type=F32)
    glu = u[:, :CONV_CH] * jax.nn.sigmoid(u[:, CONV_CH:])

    @pl.when(i == 0)
    def _():
        ext_ref[0:CONV_HALO, :] = jnp.zeros((CONV_HALO, CONV_CH), F32)

    ext_ref[CONV_HALO:CONV_HALO + tm, :] = glu
    y = jnp.broadcast_to(cb_ref[...], (tm, CONV_CH))
    base = CONV_HALO - (CONV_KERNEL - 1)
    for j in range(CONV_KERNEL):
        y = y + cw_ref[j:j + 1, :] * ext_ref[base + j:base + j + tm, :]
    ext_ref[0:CONV_HALO, :] = ext_ref[tm:tm + CONV_HALO, :]

    mu = jnp.mean(y, axis=-1, keepdims=True)
    yc = y - mu
    var = jnp.mean(yc * yc, axis=-1, keepdims=True)
    z = yc * lax.rsqrt(var + EPS) * lng_ref[...] + lnb_ref[...]
    o_ref[0] = (z * jax.nn.sigmoid(z)).astype(o_ref.dtype)


def _mix_kernel(x_ref, attn_ref, conv_ref, p_ref, gmix_ref, wga_ref, wgc_ref, wao_ref, wco_ref, wout_ref,
                gffn_ref, wfg_ref, wfu_ref, wfd_ref, gple_ref, wpg_ref, wpp_ref, o_ref, *, ff_chunk):
    x = x_ref[...]
    h = _rms_h(x, gmix_ref[...])
    ga = jax.nn.sigmoid(jnp.dot(h, wga_ref[...], preferred_element_type=F32))
    gc = jax.nn.sigmoid(jnp.dot(h, wgc_ref[...], preferred_element_type=F32))
    merged = (ga * jnp.dot(attn_ref[...], wao_ref[...], preferred_element_type=F32)
              + gc * jnp.dot(conv_ref[...], wco_ref[...], preferred_element_type=F32))
    x = x + jnp.dot(merged.astype(BF16), wout_ref[...], preferred_element_type=F32)

    hf = _rms_h(x, gffn_ref[...])
    d_ff = wfg_ref.shape[1]
    ffn = jnp.zeros(x.shape, F32)
    for c in range(d_ff // ff_chunk):
        sl = slice(c * ff_chunk, (c + 1) * ff_chunk)
        g = jnp.dot(hf, wfg_ref[:, sl], preferred_element_type=F32)
        up = jnp.dot(hf, wfu_ref[:, sl], preferred_element_type=F32)
        act = (g * jax.nn.sigmoid(g) * up).astype(BF16)
        ffn = ffn + jnp.dot(act, wfd_ref[sl, :], preferred_element_type=F32)
    x = x + ffn

    hp = _rms_h(x, gple_ref[...])
    gate = jax.nn.sigmoid(jnp.dot(hp, wpg_ref[...], preferred_element_type=F32))
    emb = jnp.dot(p_ref[...].astype(BF16), wpp_ref[...], preferred_element_type=F32)
    o_ref[...] = x + gate * emb


def _const_spec(shape):
    nd = len(shape)
    return pl.BlockSpec(shape, lambda *_: (0,) * nd, pipeline_mode=pl.Buffered(1))


def _layer(x, p, g_mix, w_in, g_q, g_k, conv_w, conv_b, conv_ln_g, conv_ln_b, w_attn_o, w_conv_o, w_out,
           g_ffn, w_ffn_gate, w_ffn_up, w_ffn_down, g_ple, w_ple_gate, w_ple_proj):
    B, S, D = x.shape
    T = B * S
    topk = min(TOPK_MAX, S // 4)
    x2 = x.reshape(T, D)

    sizes = (ATTN_WIDTH, ATTN_WIDTH, ATTN_WIDTH, N_IDX_HEADS * IDX_DIM, IDX_DIM, N_IDX_HEADS,
             2 * CONV_CH, D, D)
    offs = [0]
    for s in sizes:
        offs.append(offs[-1] + s)
    wcols = [w_in[:, offs[i]:offs[i + 1]] for i in range(len(sizes))]
    wq, wk, wv, wqi, wki, wwi, wci, wga, wgc = wcols
    wqT = wq.T.astype(BF16)
    wk = wk.astype(BF16)
    wvT = wv.T.astype(BF16)
    wqiT = wqi.T.astype(BF16)
    wki2 = jnp.concatenate([wki, wki], axis=1).astype(BF16)
    wwiT = wwi.T.astype(BF16)
    gmix = g_mix.reshape(1, D)
    q_scale = (HEAD_DIM ** -0.5) * math.log2(math.e)
    gq = (jnp.tile(g_q, N_HEADS) * q_scale).reshape(ATTN_WIDTH, 1)
    gk = jnp.tile(g_k, N_HEADS).reshape(1, ATTN_WIDTH)
    head_id = jnp.arange(ATTN_WIDTH) // HEAD_DIM
    hd = (head_id[:, None] == head_id[None, :]).astype(BF16)

    tm = 512
    ns = S // tm
    row_tile = lambda i: (i, 0)
    seq_tile = lambda i: (i // ns, 0, i % ns)
    proj = pl.pallas_call(
        _proj_kernel,
        grid=(T // tm,),
        in_specs=[
            pl.BlockSpec((tm, D), row_tile),
            _const_spec((1, D)),
            _const_spec((ATTN_WIDTH, D)), _const_spec((D, ATTN_WIDTH)), _const_spec((ATTN_WIDTH, D)),
            _const_spec((ATTN_WIDTH, D)), _const_spec((D, LANES)), _const_spec((N_IDX_HEADS, D)),
            _const_spec((ATTN_WIDTH, 1)), _const_spec((1, ATTN_WIDTH)), _const_spec((ATTN_WIDTH, ATTN_WIDTH)),
        ],
        out_specs=[
            pl.BlockSpec((1, 2 * ATTN_WIDTH, tm), seq_tile),
            pl.BlockSpec((tm, ATTN_WIDTH), row_tile),
            pl.BlockSpec((1, ATTN_WIDTH, tm), seq_tile),
            pl.BlockSpec((1, 2 * ATTN_WIDTH, tm), seq_tile),
            pl.BlockSpec((tm, LANES), row_tile),
            pl.BlockSpec((1, N_IDX_HEADS, tm), seq_tile),
        ],
        out_shape=[
            jax.ShapeDtypeStruct((B, 2 * ATTN_WIDTH, S), BF16),
            jax.ShapeDtypeStruct((T, ATTN_WIDTH), BF16),
            jax.ShapeDtypeStruct((B, ATTN_WIDTH, S), BF16),
            jax.ShapeDtypeStruct((B, 2 * ATTN_WIDTH, S), BF16),
            jax.ShapeDtypeStruct((T, LANES), BF16),
            jax.ShapeDtypeStruct((B, N_IDX_HEADS, S), F32),
        ],
        compiler_params=pltpu.CompilerParams(dimension_semantics=("arbitrary",), vmem_limit_bytes=VMEM_LIMIT),
        name="dsa_proj",
    )
    qT, k, vT, qiT, ki2, wT = proj(x2, gmix, wqT, wk, wvT, wqiT, wki2, wwiT, gq, gk, hd)

    tq = 256
    kc_score = 128
    kc_attn = 128
    assert topk <= 2 * kc_score
    resident = functools.partial(pl.BlockSpec, pipeline_mode=pl.Buffered(1))
    dsa = pl.pallas_call(
        functools.partial(_dsa_kernel, topk=topk, kc_score=kc_score, kc_count=256, kc_attn=kc_attn),
        grid=(B, S // tq),
        in_specs=[
            pl.BlockSpec((1, 2 * ATTN_WIDTH, tq), lambda b, n: (b, 0, n)),
            pl.BlockSpec((1, 2 * ATTN_WIDTH, tq), lambda b, n: (b, 0, n)),
            pl.BlockSpec((1, N_IDX_HEADS, tq), lambda b, n: (b, 0, n)),
            resident((1, S, ATTN_WIDTH), lambda b, n: (b, 0, 0)),
            resident((1, S, LANES), lambda b, n: (b, 0, 0)),
            resident((1, ATTN_WIDTH, S), lambda b, n: (b, 0, 0)),
        ],
        out_specs=pl.BlockSpec((1, tq, ATTN_WIDTH), lambda b, n: (b, n, 0)),
        out_shape=jax.ShapeDtypeStruct((B, S, ATTN_WIDTH), BF16),
        scratch_shapes=[
            pltpu.VMEM((S, tq), F32),
            pltpu.VMEM((ATTN_WIDTH, tq), F32),
            pltpu.VMEM((2, N_HEADS, kc_attn, tq), F32),
        ],
        compiler_params=pltpu.CompilerParams(dimension_semantics=("arbitrary", "arbitrary"),
                                             vmem_limit_bytes=VMEM_LIMIT),
        name="dsa_attn",
    )
    attn = dsa(qT, qiT, wT, k.reshape(B, S, ATTN_WIDTH), ki2.reshape(B, S, LANES), vT)

    tmc = 512
    cw = jnp.pad(conv_w.reshape(CONV_KERNEL, CONV_CH), ((0, CONV_HALO - CONV_KERNEL), (0, 0)))
    conv = pl.pallas_call(
        _conv_kernel,
        grid=(B, S // tmc),
        in_specs=[
            pl.BlockSpec((1, tmc, D), lambda b, i: (b, i, 0)),
            _const_spec((1, D)),
            _const_spec((D, 2 * CONV_CH)),
            _const_spec((CONV_HALO, CONV_CH)),
            _const_spec((1, CONV_CH)), _const_spec((1, CONV_CH)), _const_spec((1, CONV_CH)),
        ],
        out_specs=pl.BlockSpec((1, tmc, CONV_CH), lambda b, i: (b, i, 0)),
        out_shape=jax.ShapeDtypeStruct((B, S, CONV_CH), BF16),
        scratch_shapes=[pltpu.VMEM((tmc + CONV_HALO, CONV_CH), F32)],
        compiler_params=pltpu.CompilerParams(dimension_semantics=("arbitrary", "arbitrary"),
                                             vmem_limit_bytes=VMEM_LIMIT),
        name="conv_module",
    )(x, gmix, wci.astype(BF16), cw, conv_b.reshape(1, CONV_CH), conv_ln_g.reshape(1, CONV_CH),
      conv_ln_b.reshape(1, CONV_CH))

    tmd = 256
    d_ff = w_ffn_gate.shape[1]
    ple = p.shape[-1]
    ff_chunk = 256 if d_ff % 256 == 0 else d_ff
    out = pl.pallas_call(
        functools.partial(_mix_kernel, ff_chunk=ff_chunk),
        grid=(T // tmd,),
        in_specs=[
            pl.BlockSpec((tmd, D), row_tile),
            pl.BlockSpec((tmd, ATTN_WIDTH), row_tile),
            pl.BlockSpec((tmd, CONV_CH), row_tile),
            pl.BlockSpec((tmd, ple), row_tile),
            _const_spec((1, D)),
            _const_spec((D, D)), _const_spec((D, D)),
            _const_spec((ATTN_WIDTH, D)), _const_spec((CONV_CH, D)), _const_spec((D, D)),
            _const_spec((1, D)),
            _const_spec((D, d_ff)), _const_spec((D, d_ff)), _const_spec((d_ff, D)),
            _const_spec((1, D)),
            _const_spec((D, D)), _const_spec((ple, D)),
        ],
        out_specs=pl.BlockSpec((tmd, D), row_tile),
        out_shape=jax.ShapeDtypeStruct((T, D), F32),
        compiler_params=pltpu.CompilerParams(dimension_semantics=("arbitrary",), vmem_limit_bytes=VMEM_LIMIT),
        name="mix_ffn_ple",
    )(x2, attn.reshape(T, ATTN_WIDTH), conv.reshape(T, CONV_CH), p.reshape(T, ple),
      gmix, wga.astype(BF16), wgc.astype(BF16), w_attn_o.astype(BF16), w_conv_o.astype(BF16),
      w_out.astype(BF16), g_ffn.reshape(1, D), w_ffn_gate.astype(BF16), w_ffn_up.astype(BF16),
      w_ffn_down.astype(BF16), g_ple.reshape(1, D), w_ple_gate.astype(BF16), w_ple_proj.astype(BF16))
    return out.reshape(B, S, D)


def kernel(x, p, g_mix, w_in, g_q, g_k, conv_w, conv_b, conv_ln_g, conv_ln_b, w_attn_o, w_conv_o, w_out,
           g_ffn, w_ffn_gate, w_ffn_up, w_ffn_down, g_ple, w_ple_gate, w_ple_proj):
    depth = w_in.shape[0]
    for i in range(depth):
        x = _layer(x, p[i], g_mix[i], w_in[i], g_q[i], g_k[i], conv_w[i], conv_b[i], conv_ln_g[i],
                   conv_ln_b[i], w_attn_o[i], w_conv_o[i], w_out[i], g_ffn[i], w_ffn_gate[i],
                   w_ffn_up[i], w_ffn_down[i], g_ple[i], w_ple_gate[i], w_ple_proj[i])
    return x
```

```python
import functools
import math

import jax
import jax.numpy as jnp
from jax import lax
from jax.experimental import pallas as pl
from jax.experimental.pallas import tpu as pltpu

F32 = jnp.float32
BF16 = jnp.bfloat16
I32 = jnp.int32

N_HEADS = 8
HEAD_DIM = 64
ATTN_WIDTH = N_HEADS * HEAD_DIM
N_IDX_HEADS = 8
IDX_DIM = 64
TOPK_MAX = 256
CONV_CH = 512
CONV_KERNEL = 31
EPS = 1e-6

LANES = 128
SUBLANES = 8
CONV_HALO = 32
CONV_ROWS = 32
VMEM_LIMIT = 56 * 1024 * 1024

KEY_LOWEST = -2139095040
KEY_INF = 2139095040
MASK_VAL = -1e30
TINY_NORMAL = 1e-30
BRACKET_SMALL = 4.0
STAGNANT_PASSES = 2.0
BLIND_PASSES = 6
KEY_MID_EVERY = 4

NT_DIMS = (((1,), (1,)), ((), ()))


def _key_to_f32(k):
    bits = k ^ ((k >> 31) & 0x7FFFFFFF)
    return lax.bitcast_convert_type(bits, F32)


def _f32_to_key(x):
    bits = lax.bitcast_convert_type(x, I32)
    return bits ^ ((bits >> 31) & 0x7FFFFFFF)


def _rms_h(x, g):
    ms = jnp.mean(x * x, axis=-1, keepdims=True)
    return (x * lax.rsqrt(ms + EPS) * g).astype(BF16)


def _split_bf16(a):
    hi = a.astype(BF16)
    lo = (a - hi.astype(F32)).astype(BF16)
    return hi, lo


def _proj_kernel(x_ref, gmix_ref, wqT_ref, wk_ref, wvT_ref, wqiT_ref, wki_ref, wwiT_ref,
                 gq_ref, gk_ref, hd_ref,
                 qT_out, k_out, vT_out, qiT_out, ki_out, wT_out):
    tm = x_ref.shape[0]
    h = _rms_h(x_ref[...], gmix_ref[...])
    hd = hd_ref[...]

    row = lax.broadcasted_iota(I32, (LANES, tm), 0)
    top_half = row < HEAD_DIM

    def expand_heads(yT, out_ref):
        for pair in range(N_HEADS // 2):
            blk = yT[pair * LANES:(pair + 1) * LANES, :]
            out_ref[0, (2 * pair) * LANES:(2 * pair + 1) * LANES, :] = jnp.where(top_half, blk, 0.0).astype(BF16)
            out_ref[0, (2 * pair + 1) * LANES:(2 * pair + 2) * LANES, :] = jnp.where(top_half, 0.0, blk).astype(BF16)

    qT = lax.dot_general(wqT_ref[...], h, NT_DIMS, preferred_element_type=F32)
    hi, lo = _split_bf16(qT * qT)
    ssq = jnp.dot(hd, hi, preferred_element_type=F32) + jnp.dot(hd, lo, preferred_element_type=F32)
    expand_heads(qT * lax.rsqrt(ssq * (1.0 / HEAD_DIM) + EPS) * gq_ref[...], qT_out)

    k = jnp.dot(h, wk_ref[...], preferred_element_type=F32)
    hi, lo = _split_bf16(k * k)
    ssk = jnp.dot(hi, hd, preferred_element_type=F32) + jnp.dot(lo, hd, preferred_element_type=F32)
    k_out[...] = (k * lax.rsqrt(ssk * (1.0 / HEAD_DIM) + EPS) * gk_ref[...]).astype(BF16)

    vT_out[0] = lax.dot_general(wvT_ref[...], h, NT_DIMS, preferred_element_type=F32).astype(BF16)

    qiT = lax.dot_general(wqiT_ref[...], h, NT_DIMS, preferred_element_type=F32) * (IDX_DIM ** -0.5)
    expand_heads(qiT, qiT_out)

    ki_out[...] = jnp.dot(h, wki_ref[...], preferred_element_type=F32).astype(BF16)
    wT_out[0] = (lax.dot_general(wwiT_ref[...], h, NT_DIMS, preferred_element_type=F32)
                 * (N_IDX_HEADS ** -0.5))


def _dsa_kernel(qT_ref, qiT_ref, wT_ref, k_ref, ki_ref, vT_ref, o_ref,
                sT_ref, accT_ref, lg_ref, *, topk, kc_score, kc_count, kc_attn):
    tq = qT_ref.shape[2]
    n = pl.program_id(1)
    t0 = n * tq
    n_keys = t0 + tq
    qpos = t0 + lax.broadcasted_iota(I32, (1, tq), 1)
    kf = float(topk)

    def score_chunk(j, masked):
        off = pl.multiple_of(j * kc_score, kc_score)
        ki = ki_ref[0, pl.ds(off, kc_score), :]
        acc = jnp.zeros((kc_score, tq), F32)
        for hh in range(N_IDX_HEADS):
            z = jnp.dot(ki, qiT_ref[0, hh * LANES:(hh + 1) * LANES, :], preferred_element_type=F32)
            acc = acc + jnp.maximum(z, 0.0) * wT_ref[0, hh:hh + 1, :]
        if masked:
            kpos = off + lax.broadcasted_iota(I32, (kc_score, tq), 0)
            acc = jnp.where(kpos <= qpos, acc, -jnp.inf)
        sT_ref[pl.ds(off, kc_score), :] = acc
        return acc

    def chunk_pair(i, gm):
        g0, g1 = gm
        return (jnp.maximum(g0, score_chunk(2 * i, False)), jnp.maximum(g1, score_chunk(2 * i + 1, False)))

    n_pairs = t0 // (2 * kc_score)
    ninf = jnp.full((kc_score, tq), -jnp.inf, F32)
    g0, g1 = lax.fori_loop(0, n_pairs, chunk_pair, (ninf, ninf))
    for d in range(tq // (2 * kc_score)):
        g0 = jnp.maximum(g0, score_chunk(2 * (n_pairs + d), True))
        g1 = jnp.maximum(g1, score_chunk(2 * (n_pairs + d) + 1, True))
    grp_min = jnp.min(jnp.minimum(g0, g1), axis=0, keepdims=True)
    row_max = jnp.max(jnp.maximum(g0, g1), axis=0, keepdims=True)

    n_cnt = n_keys // kc_count

    def sweep(init, body_fn):
        def chunk(acc, off):
            return body_fn(acc, sT_ref[pl.ds(pl.multiple_of(off, kc_count), kc_count), :])

        def body(j, acc):
            return chunk(chunk(acc, 2 * j * kc_count), (2 * j + 1) * kc_count)

        acc = lax.fori_loop(0, n_cnt // 2, body, init)
        return lax.cond(n_cnt % 2 == 1, lambda a: chunk(a, (n_cnt - 1) * kc_count), lambda a: a, acc)

    def fold(x, op):
        return op(x.reshape(kc_count // SUBLANES, SUBLANES, tq), axis=0)

    def count_ge(t_row):
        acc = sweep(jnp.zeros((SUBLANES, tq), F32),
                    lambda a, s: a + fold(jnp.where(s >= t_row, 1.0, 0.0), jnp.sum))
        return jnp.sum(acc, axis=0, keepdims=True)

    def next_above(t_row):
        def body(acc, s):
            c, mn = acc
            gt = s > t_row
            return (c + fold(jnp.where(gt, 1.0, 0.0), jnp.sum),
                    jnp.minimum(mn, fold(jnp.where(gt, s, jnp.inf), jnp.min)))
        c, mn = sweep((jnp.zeros((SUBLANES, tq), F32), jnp.full((SUBLANES, tq), jnp.inf, F32)), body)
        return jnp.sum(c, axis=0, keepdims=True), jnp.min(mn, axis=0, keepdims=True)

    def active_queries(lo, hi, cnt_lo):
        return (cnt_lo > kf) & ((hi - lo) != 1)

    def n_true(mask):
        return jnp.sum(jnp.where(mask, 1.0, 0.0))

    few = qpos < 2 * kc_score
    lo0 = jnp.where(few, KEY_LOWEST, _f32_to_key(grp_min))
    hi0 = _f32_to_key(jnp.maximum(2.0 * row_max, TINY_NORMAL))
    cnt_lo0 = jnp.where(few, (qpos + 1).astype(F32), count_ge(_key_to_f32(lo0)))
    cnt_hi0 = jnp.zeros((1, tq), F32)

    def a_step(it, st):
        lo, hi, cnt_lo, cnt_hi, same = st
        act = active_queries(lo, hi, cnt_lo)
        key_mid = lo + lax.shift_right_logical(hi - lo, jnp.ones_like(lo))
        val_mid = 0.5 * _key_to_f32(lo) + 0.5 * _key_to_f32(hi)
        val_key = jnp.minimum(jnp.maximum(_f32_to_key(val_mid), lo + 1), hi - 1)
        use_val = (it % KEY_MID_EVERY != KEY_MID_EVERY - 1) & (jnp.abs(val_mid) < jnp.inf)
        mid = jnp.where(use_val, val_key, key_mid)
        c = count_ge(_key_to_f32(mid))
        ge = c >= kf
        take_lo = act & ge
        take_hi = act & jnp.logical_not(ge)
        new_lo = jnp.where(take_lo, c, cnt_lo)
        new_hi = jnp.where(take_hi, c, cnt_hi)
        same = jnp.where((new_lo - new_hi) == (cnt_lo - cnt_hi), same + 1.0, 0.0)
        return jnp.where(take_lo, mid, lo), jnp.where(take_hi, mid, hi), new_lo, new_hi, same

    def n_wide(st):
        lo, hi, cnt_lo, cnt_hi, same = st
        wide = (active_queries(lo, hi, cnt_lo) & ((cnt_lo - cnt_hi) > BRACKET_SMALL)
                & (same < STAGNANT_PASSES))
        return n_true(wide)

    def a_body(carry):
        _, it, st = carry
        st = a_step(it, st)
        return n_wide(st), it + 1, st

    st = (lo0, hi0, cnt_lo0, cnt_hi0, jnp.zeros((1, tq), F32))
    st = lax.fori_loop(0, BLIND_PASSES, a_step, st)
    _, _, (lo, hi, cnt_lo, cnt_hi, _) = lax.while_loop(
        lambda c: c[0] > 0.0, a_body, (n_wide(st), jnp.int32(BLIND_PASSES), st))

    def b_body(st):
        _, lo, cnt_lo, cnt_gt, fin = st
        pend = (cnt_lo > kf) & (fin == 0.0)
        c_gt, nxt = next_above(_key_to_f32(lo))
        done = pend & (c_gt < kf)
        step = pend & jnp.logical_not(done)
        cnt_gt = jnp.where(done, c_gt, cnt_gt)
        fin = jnp.where(done, 1.0, fin)
        lo = jnp.where(step, _f32_to_key(nxt), lo)
        cnt_lo = jnp.where(step, c_gt, cnt_lo)
        return (n_true(step & (cnt_lo > kf)), lo, cnt_lo, cnt_gt, fin)

    zeros = jnp.zeros((1, tq), F32)
    _, lo_b, cnt_lo_b, cnt_gt, _ = lax.while_loop(
        lambda st: st[0] > 0.0, b_body, (n_true(cnt_lo > kf), lo, cnt_lo, zeros, zeros))

    t_lo = _key_to_f32(lo_b)
    need = jnp.where(cnt_lo_b > kf, kf - cnt_gt, jnp.inf)

    accT_ref[...] = jnp.zeros(accT_ref.shape, F32)
    tri = (lax.broadcasted_iota(I32, (kc_attn, kc_attn), 0)
           >= lax.broadcasted_iota(I32, (kc_attn, kc_attn), 1)).astype(BF16)

    n_att = n_keys // kc_attn

    def logits_stage(j, slot, ties_seen):
        off = pl.multiple_of(j * kc_attn, kc_attn)
        s = sT_ref[pl.ds(off, kc_attn), :]
        is_eq = s == t_lo
        rank = jnp.dot(tri, jnp.where(is_eq, 1.0, 0.0).astype(BF16), preferred_element_type=F32)
        sel = (s > t_lo) | (is_eq & (rank + ties_seen <= need))
        bias = jnp.where(sel, 0.0, MASK_VAL)
        cmax = []
        for hh in range(N_HEADS):
            pair = hh // 2
            kp = k_ref[0, pl.ds(off, kc_attn), pair * LANES:(pair + 1) * LANES]
            lg = jnp.dot(kp, qT_ref[0, hh * LANES:(hh + 1) * LANES, :], preferred_element_type=F32) + bias
            lg_ref[slot, hh] = lg
            cmax.append(jnp.max(lg, axis=0, keepdims=True))
        return jnp.concatenate(cmax, axis=0), ties_seen + rank[kc_attn - 1:kc_attn, :]

    def attn_chunk(j, carry):
        m_old, l_old, cmax, ties_seen = carry
        off = pl.multiple_of(j * kc_attn, kc_attn)
        slot = j % 2
        m_new = jnp.maximum(m_old, cmax)
        alpha = jnp.exp2(m_old - m_new)
        psum = []
        for hh in range(N_HEADS):
            p = jnp.exp2(lg_ref[slot, hh] - m_new[hh:hh + 1, :])
            psum.append(jnp.sum(p, axis=0, keepdims=True))
            hs = slice(hh * HEAD_DIM, (hh + 1) * HEAD_DIM)
            pv = jnp.dot(vT_ref[0, hs, pl.ds(off, kc_attn)], p.astype(BF16), preferred_element_type=F32)
            accT_ref[hs, :] = alpha[hh:hh + 1, :] * accT_ref[hs, :] + pv
        cmax_next, ties_seen = logits_stage(jnp.minimum(j + 1, n_att - 1), 1 - slot, ties_seen)
        return m_new, alpha * l_old + jnp.concatenate(psum, axis=0), cmax_next, ties_seen

    m0 = jnp.full((N_HEADS, tq), MASK_VAL, F32)
    l0 = jnp.zeros((N_HEADS, tq), F32)
    cmax0, ties0 = logits_stage(0, 0, jnp.zeros((1, tq), F32))
    _, lsum, _, _ = lax.fori_loop(0, n_att, attn_chunk, (m0, l0, cmax0, ties0))

    for hh in range(N_HEADS):
        hs = slice(hh * HEAD_DIM, (hh + 1) * HEAD_DIM)
        accT_ref[hs, :] = accT_ref[hs, :] / lsum[hh:hh + 1, :]
    o_ref[0] = accT_ref[...].T.astype(o_ref.dtype)


def _conv_kernel(x_ref, gmix_ref, wci_ref, cw_ref, cb_ref, lng_ref, lnb_ref, o_ref, ext_ref, sh_ref, y_ref):
    tm = x_ref.shape[1]
    i = pl.program_id(1)
    h = _rms_h(x_ref[0], gmix_ref[...])
    u = jnp.dot(h, wci_ref[...], preferred_element_type=F32)

    @pl.when(i == 0)
    def _():
        ext_ref[0:CONV_HALO, :] = jnp.zeros((CONV_HALO, CONV_CH), F32)

    ext_ref[CONV_HALO:CONV_HALO + tm, :] = u[:, :CONV_CH] * jax.nn.sigmoid(u[:, CONV_CH:])
    n_sh = tm + CONV_HALO - SUBLANES
    for b in range(1, SUBLANES):
        sh_ref[b - 1] = ext_ref[b:b + n_sh, :]

    base = CONV_HALO - (CONV_KERNEL - 1)

    def row_block(r, carry):
        r0 = pl.multiple_of(r * CONV_ROWS, CONV_ROWS)
        y = jnp.broadcast_to(cb_ref[...], (CONV_ROWS, CONV_CH))
        for j in range(CONV_KERNEL):
            a, b = divmod(base + j, SUBLANES)
            src = ext_ref if b == 0 else sh_ref.at[b - 1]
            y = y + cw_ref[j:j + 1, :] * src[pl.ds(r0 + a * SUBLANES, CONV_ROWS), :]
        y_ref[pl.ds(r0, CONV_ROWS), :] = y
        return carry

    lax.fori_loop(0, tm // CONV_ROWS, row_block, 0)
    y = y_ref[...]
    mu = jnp.mean(y, axis=-1, keepdims=True)
    yc = y - mu
    var = jnp.mean(yc * yc, axis=-1, keepdims=True)
    z = yc * lax.rsqrt(var + EPS) * lng_ref[...] + lnb_ref[...]
    o_ref[0] = (z * jax.nn.sigmoid(z)).astype(o_ref.dtype)
    ext_ref[0:CONV_HALO, :] = ext_ref[tm:tm + CONV_HALO, :]


def _mix_kernel(x_ref, attn_ref, conv_ref, p_ref, gmix_ref, wga_ref, wgc_ref, wao_ref, wco_ref, wout_ref,
                gffn_ref, wfg_ref, wfu_ref, wfd_ref, gple_ref, wpg_ref, wpp_ref, o_ref, *, ff_chunk):
    x = x_ref[...]
    h = _rms_h(x, gmix_ref[...])
    ga = jax.nn.sigmoid(jnp.dot(h, wga_ref[...], preferred_element_type=F32))
    gc = jax.nn.sigmoid(jnp.dot(h, wgc_ref[...], preferred_element_type=F32))
    merged = (ga * jnp.dot(attn_ref[...], wao_ref[...], preferred_element_type=F32)
              + gc * jnp.dot(conv_ref[...], wco_ref[...], preferred_element_type=F32))
    x = x + jnp.dot(merged.astype(BF16), wout_ref[...], preferred_element_type=F32)

    hf = _rms_h(x, gffn_ref[...])
    d_ff = wfg_ref.shape[1]
    ffn = jnp.zeros(x.shape, F32)
    for c in range(d_ff // ff_chunk):
        sl = slice(c * ff_chunk, (c + 1) * ff_chunk)
        g = jnp.dot(hf, wfg_ref[:, sl], preferred_element_type=F32)
        up = jnp.dot(hf, wfu_ref[:, sl], preferred_element_type=F32)
        act = (g * jax.nn.sigmoid(g) * up).astype(BF16)
        ffn = ffn + jnp.dot(act, wfd_ref[sl, :], preferred_element_type=F32)
    x = x + ffn

    hp = _rms_h(x, gple_ref[...])
    gate = jax.nn.sigmoid(jnp.dot(hp, wpg_ref[...], preferred_element_type=F32))
    emb = jnp.dot(p_ref[...].astype(BF16), wpp_ref[...], preferred_element_type=F32)
    o_ref[...] = x + gate * emb


def _const_spec(shape):
    nd = len(shape)
    return pl.BlockSpec(shape, lambda *_: (0,) * nd, pipeline_mode=pl.Buffered(1))


def _layer(x, p, g_mix, w_in, g_q, g_k, conv_w, conv_b, conv_ln_g, conv_ln_b, w_attn_o, w_conv_o, w_out,
           g_ffn, w_ffn_gate, w_ffn_up, w_ffn_down, g_ple, w_ple_gate, w_ple_proj):
    B, S, D = x.shape
    T = B * S
    topk = min(TOPK_MAX, S // 4)
    x2 = x.reshape(T, D)

    sizes = (ATTN_WIDTH, ATTN_WIDTH, ATTN_WIDTH, N_IDX_HEADS * IDX_DIM, IDX_DIM, N_IDX_HEADS,
             2 * CONV_CH, D, D)
    offs = [0]
    for s in sizes:
        offs.append(offs[-1] + s)
    wcols = [w_in[:, offs[i]:offs[i + 1]] for i in range(len(sizes))]
    wq, wk, wv, wqi, wki, wwi, wci, wga, wgc = wcols
    wqT = wq.T.astype(BF16)
    wk = wk.astype(BF16)
    wvT = wv.T.astype(BF16)
    wqiT = wqi.T.astype(BF16)
    wki2 = jnp.concatenate([wki, wki], axis=1).astype(BF16)
    wwiT = wwi.T.astype(BF16)
    gmix = g_mix.reshape(1, D)
    q_scale = (HEAD_DIM ** -0.5) * math.log2(math.e)
    gq = (jnp.tile(g_q, N_HEADS) * q_scale).reshape(ATTN_WIDTH, 1)
    gk = jnp.tile(g_k, N_HEADS).reshape(1, ATTN_WIDTH)
    head_id = jnp.arange(ATTN_WIDTH) // HEAD_DIM
    hd = (head_id[:, None] == head_id[None, :]).astype(BF16)

    tm = 512
    ns = S // tm
    row_tile = lambda i: (i, 0)
    seq_tile = lambda i: (i // ns, 0, i % ns)
    proj = pl.pallas_call(
        _proj_kernel,
        grid=(T // tm,),
        in_specs=[
            pl.BlockSpec((tm, D), row_tile),
            _const_spec((1, D)),
            _const_spec((ATTN_WIDTH, D)), _const_spec((D, ATTN_WIDTH)), _const_spec((ATTN_WIDTH, D)),
            _const_spec((ATTN_WIDTH, D)), _const_spec((D, LANES)), _const_spec((N_IDX_HEADS, D)),
            _const_spec((ATTN_WIDTH, 1)), _const_spec((1, ATTN_WIDTH)), _const_spec((ATTN_WIDTH, ATTN_WIDTH)),
        ],
        out_specs=[
            pl.BlockSpec((1, 2 * ATTN_WIDTH, tm), seq_tile),
            pl.BlockSpec((tm, ATTN_WIDTH), row_tile),
            pl.BlockSpec((1, ATTN_WIDTH, tm), seq_tile),
            pl.BlockSpec((1, 2 * ATTN_WIDTH, tm), seq_tile),
            pl.BlockSpec((tm, LANES), row_tile),
            pl.BlockSpec((1, N_IDX_HEADS, tm), seq_tile),
        ],
        out_shape=[
            jax.ShapeDtypeStruct((B, 2 * ATTN_WIDTH, S), BF16),
            jax.ShapeDtypeStruct((T, ATTN_WIDTH), BF16),
            jax.ShapeDtypeStruct((B, ATTN_WIDTH, S), BF16),
            jax.ShapeDtypeStruct((B, 2 * ATTN_WIDTH, S), BF16),
            jax.ShapeDtypeStruct((T, LANES), BF16),
            jax.ShapeDtypeStruct((B, N_IDX_HEADS, S), F32),
        ],
        compiler_params=pltpu.CompilerParams(dimension_semantics=("arbitrary",), vmem_limit_bytes=VMEM_LIMIT),
        name="dsa_proj",
    )
    qT, k, vT, qiT, ki2, wT = proj(x2, gmix, wqT, wk, wvT, wqiT, wki2, wwiT, gq, gk, hd)

    tq = 256
    kc_score = 128
    kc_attn = 128
    assert topk <= 2 * kc_score
    resident = functools.partial(pl.BlockSpec, pipeline_mode=pl.Buffered(1))
    dsa = pl.pallas_call(
        functools.partial(_dsa_kernel, topk=topk, kc_score=kc_score, kc_count=256, kc_attn=kc_attn),
        grid=(B, S // tq),
        in_specs=[
            pl.BlockSpec((1, 2 * ATTN_WIDTH, tq), lambda b, n: (b, 0, n)),
            pl.BlockSpec((1, 2 * ATTN_WIDTH, tq), lambda b, n: (b, 0, n)),
            pl.BlockSpec((1, N_IDX_HEADS, tq), lambda b, n: (b, 0, n)),
            resident((1, S, ATTN_WIDTH), lambda b, n: (b, 0, 0)),
            resident((1, S, LANES), lambda b, n: (b, 0, 0)),
            resident((1, ATTN_WIDTH, S), lambda b, n: (b, 0, 0)),
        ],
        out_specs=pl.BlockSpec((1, tq, ATTN_WIDTH), lambda b, n: (b, n, 0)),
        out_shape=jax.ShapeDtypeStruct((B, S, ATTN_WIDTH), BF16),
        scratch_shapes=[
            pltpu.VMEM((S, tq), F32),
            pltpu.VMEM((ATTN_WIDTH, tq), F32),
            pltpu.VMEM((2, N_HEADS, kc_attn, tq), F32),
        ],
        compiler_params=pltpu.CompilerParams(dimension_semantics=("arbitrary", "arbitrary"),
                                             vmem_limit_bytes=VMEM_LIMIT),
        name="dsa_attn",
    )
    attn = dsa(qT, qiT, wT, k.reshape(B, S, ATTN_WIDTH), ki2.reshape(B, S, LANES), vT)

    tmc = 512
    cw = jnp.pad(conv_w.reshape(CONV_KERNEL, CONV_CH), ((0, CONV_HALO - CONV_KERNEL), (0, 0)))
    conv = pl.pallas_call(
        _conv_kernel,
        grid=(B, S // tmc),
        in_specs=[
            pl.BlockSpec((1, tmc, D), lambda b, i: (b, i, 0)),
            _const_spec((1, D)),
            _const_spec((D, 2 * CONV_CH)),
            _const_spec((CONV_HALO, CONV_CH)),
            _const_spec((1, CONV_CH)), _const_spec((1, CONV_CH)), _const_spec((1, CONV_CH)),
        ],
        out_specs=pl.BlockSpec((1, tmc, CONV_CH), lambda b, i: (b, i, 0)),
        out_shape=jax.ShapeDtypeStruct((B, S, CONV_CH), BF16),
        scratch_shapes=[pltpu.VMEM((tmc + CONV_HALO, CONV_CH), F32),
                        pltpu.VMEM((SUBLANES - 1, tmc + CONV_HALO - SUBLANES, CONV_CH), F32),
                        pltpu.VMEM((tmc, CONV_CH), F32)],
        compiler_params=pltpu.CompilerParams(dimension_semantics=("arbitrary", "arbitrary"),
                                             vmem_limit_bytes=VMEM_LIMIT),
        name="conv_module",
    )(x, gmix, wci.astype(BF16), cw, conv_b.reshape(1, CONV_CH), conv_ln_g.reshape(1, CONV_CH),
      conv_ln_b.reshape(1, CONV_CH))

    tmd = 256
    d_ff = w_ffn_gate.shape[1]
    ple = p.shape[-1]
    ff_chunk = 256 if d_ff % 256 == 0 else d_ff
    out = pl.pallas_call(
        functools.partial(_mix_kernel, ff_chunk=ff_chunk),
        grid=(T // tmd,),
        in_specs=[
            pl.BlockSpec((tmd, D), row_tile),
            pl.BlockSpec((tmd, ATTN_WIDTH), row_tile),
            pl.BlockSpec((tmd, CONV_CH), row_tile),
            pl.BlockSpec((tmd, ple), row_tile),
            _const_spec((1, D)),
            _const_spec((D, D)), _const_spec((D, D)),
            _const_spec((ATTN_WIDTH, D)), _const_spec((CONV_CH, D)), _const_spec((D, D)),
            _const_spec((1, D)),
            _const_spec((D, d_ff)), _const_spec((D, d_ff)), _const_spec((d_ff, D)),
            _const_spec((1, D)),
            _const_spec((D, D)), _const_spec((ple, D)),
        ],
        out_specs=pl.BlockSpec((tmd, D), row_tile),
        out_shape=jax.ShapeDtypeStruct((T, D), F32),
        compiler_params=pltpu.CompilerParams(dimension_semantics=("arbitrary",), vmem_limit_bytes=VMEM_LIMIT),
        name="mix_ffn_ple",
    )(x2, attn.reshape(T, ATTN_WIDTH), conv.reshape(T, CONV_CH), p.reshape(T, ple),
      gmix, wga.astype(BF16), wgc.astype(BF16), w_attn_o.astype(BF16), w_conv_o.astype(BF16),
      w_out.astype(BF16), g_ffn.reshape(1, D), w_ffn_gate.astype(BF16), w_ffn_up.astype(BF16),
      w_ffn_down.astype(BF16), g_ple.reshape(1, D), w_ple_gate.astype(BF16), w_ple_proj.astype(BF16))
    return out.reshape(B, S, D)


def kernel(x, p, g_mix, w_in, g_q, g_k, conv_w, conv_b, conv_ln_g, conv_ln_b, w_attn_o, w_conv_o, w_out,
           g_ffn, w_ffn_gate, w_ffn_up, w_ffn_down, g_ple, w_ple_gate, w_ple_proj):
    depth = w_in.shape[0]
    for i in range(depth):
        x = _layer(x, p[i], g_mix[i], w_in[i], g_q[i], g_k[i], conv_w[i], conv_b[i], conv_ln_g[i],
                   conv_ln_b[i], w_attn_o[i], w_conv_o[i], w_out[i], g_ffn[i], w_ffn_gate[i],
                   w_ffn_up[i], w_ffn_down[i], g_ple[i], w_ple_gate[i], w_ple_proj[i])
    return x
```

```python
import functools
import math

import jax
import jax.numpy as jnp
from jax import lax
from jax.experimental import pallas as pl
from jax.experimental.pallas import tpu as pltpu

F32 = jnp.float32
BF16 = jnp.bfloat16
I32 = jnp.int32

N_HEADS = 8
HEAD_DIM = 64
ATTN_WIDTH = N_HEADS * HEAD_DIM
N_IDX_HEADS = 8
IDX_DIM = 64
TOPK_MAX = 256
CONV_CH = 512
CONV_KERNEL = 31
EPS = 1e-6

LANES = 128
SUBLANES = 8
CONV_HALO = 32
CONV_ROWS = 32
VMEM_LIMIT = 56 * 1024 * 1024

KEY_LOWEST = -2139095040
KEY_INF = 2139095040
MASK_VAL = -1e30
TINY_NORMAL = 1e-30
BRACKET_SMALL = 4.0
STAGNANT_PASSES = 2.0
BLIND_PASSES = 6
KEY_MID_EVERY = 4

NT_DIMS = (((1,), (1,)), ((), ()))


def _key_to_f32(k):
    bits = k ^ ((k >> 31) & 0x7FFFFFFF)
    return lax.bitcast_convert_type(bits, F32)


def _f32_to_key(x):
    bits = lax.bitcast_convert_type(x, I32)
    return bits ^ ((bits >> 31) & 0x7FFFFFFF)


def _rms_h(x, g):
    ms = jnp.mean(x * x, axis=-1, keepdims=True)
    return (x * lax.rsqrt(ms + EPS) * g).astype(BF16)


def _split_bf16(a):
    hi = a.astype(BF16)
    lo = (a - hi.astype(F32)).astype(BF16)
    return hi, lo


def _proj_kernel(x_ref, gmix_ref, wqT_ref, wk_ref, wvT_ref, wqiT_ref, wki_ref, wwiT_ref,
                 gq_ref, gk_ref, hd_ref,
                 qT_out, k_out, vT_out, qiT_out, ki_out, wT_out):
    tm = x_ref.shape[0]
    h = _rms_h(x_ref[...], gmix_ref[...])
    hd = hd_ref[...]

    row = lax.broadcasted_iota(I32, (LANES, tm), 0)
    top_half = row < HEAD_DIM

    def expand_heads(yT, out_ref):
        for pair in range(N_HEADS // 2):
            blk = yT[pair * LANES:(pair + 1) * LANES, :]
            out_ref[0, (2 * pair) * LANES:(2 * pair + 1) * LANES, :] = jnp.where(top_half, blk, 0.0).astype(BF16)
            out_ref[0, (2 * pair + 1) * LANES:(2 * pair + 2) * LANES, :] = jnp.where(top_half, 0.0, blk).astype(BF16)

    qT = lax.dot_general(wqT_ref[...], h, NT_DIMS, preferred_element_type=F32)
    hi, lo = _split_bf16(qT * qT)
    ssq = jnp.dot(hd, hi, preferred_element_type=F32) + jnp.dot(hd, lo, preferred_element_type=F32)
    expand_heads(qT * lax.rsqrt(ssq * (1.0 / HEAD_DIM) + EPS) * gq_ref[...], qT_out)

    k = jnp.dot(h, wk_ref[...], preferred_element_type=F32)
    hi, lo = _split_bf16(k * k)
    ssk = jnp.dot(hi, hd, preferred_element_type=F32) + jnp.dot(lo, hd, preferred_element_type=F32)
    k_out[...] = (k * lax.rsqrt(ssk * (1.0 / HEAD_DIM) + EPS) * gk_ref[...]).astype(BF16)

    vT_out[0] = lax.dot_general(wvT_ref[...], h, NT_DIMS, preferred_element_type=F32).astype(BF16)

    qiT = lax.dot_general(wqiT_ref[...], h, NT_DIMS, preferred_element_type=F32) * (IDX_DIM ** -0.5)
    expand_heads(qiT, qiT_out)

    ki_out[...] = jnp.dot(h, wki_ref[...], preferred_element_type=F32).astype(BF16)
    wT_out[0] = (lax.dot_general(wwiT_ref[...], h, NT_DIMS, preferred_element_type=F32)
                 * (N_IDX_HEADS ** -0.5))


def _dsa_kernel(qT_ref, qiT_ref, wT_ref, k_ref, ki_ref, vT_ref, o_ref,
                sT_ref, accT_ref, lg_ref, *, topk, kc_score, kc_count, kc_attn):
    tq = qT_ref.shape[2]
    n = pl.program_id(1)
    t0 = n * tq
    n_keys = t0 + tq
    qpos = t0 + lax.broadcasted_iota(I32, (1, tq), 1)
    kf = float(topk)

    def score_chunk(j, masked):
        off = pl.multiple_of(j * kc_score, kc_score)
        ki = ki_ref[0, pl.ds(off, kc_score), :]
        acc = jnp.zeros((kc_score, tq), F32)
        for hh in range(N_IDX_HEADS):
            z = jnp.dot(ki, qiT_ref[0, hh * LANES:(hh + 1) * LANES, :], preferred_element_type=F32)
            acc = acc + jnp.maximum(z, 0.0) * wT_ref[0, hh:hh + 1, :]
        if masked:
            kpos = off + lax.broadcasted_iota(I32, (kc_score, tq), 0)
            acc = jnp.where(kpos <= qpos, acc, -jnp.inf)
        sT_ref[pl.ds(off, kc_score), :] = acc
        return acc

    def full_chunk(j, g):
        return jnp.maximum(g, score_chunk(j, False))

    n_full = t0 // kc_score
    g = lax.fori_loop(0, n_full, full_chunk, jnp.full((kc_score, tq), -jnp.inf, F32))
    for d in range(tq // kc_score):
        g = jnp.maximum(g, score_chunk(n_full + d, True))
    grp_min = jnp.min(g, axis=0, keepdims=True)
    row_max = jnp.max(g, axis=0, keepdims=True)

    n_cnt = n_keys // kc_count

    def sweep(init, body_fn):
        def chunk(acc, off):
            return body_fn(acc, sT_ref[pl.ds(pl.multiple_of(off, kc_count), kc_count), :])

        def body(j, acc):
            return chunk(chunk(acc, 2 * j * kc_count), (2 * j + 1) * kc_count)

        acc = lax.fori_loop(0, n_cnt // 2, body, init)
        return lax.cond(n_cnt % 2 == 1, lambda a: chunk(a, (n_cnt - 1) * kc_count), lambda a: a, acc)

    def fold(x, op):
        return op(x.reshape(kc_count // SUBLANES, SUBLANES, tq), axis=0)

    def count_ge(t_row):
        acc = sweep(jnp.zeros((SUBLANES, tq), F32),
                    lambda a, s: a + fold(jnp.where(s >= t_row, 1.0, 0.0), jnp.sum))
        return jnp.sum(acc, axis=0, keepdims=True)

    def next_above(t_row):
        def body(acc, s):
            c, mn = acc
            gt = s > t_row
            return (c + fold(jnp.where(gt, 1.0, 0.0), jnp.sum),
                    jnp.minimum(mn, fold(jnp.where(gt, s, jnp.inf), jnp.min)))
        c, mn = sweep((jnp.zeros((SUBLANES, tq), F32), jnp.full((SUBLANES, tq), jnp.inf, F32)), body)
        return jnp.sum(c, axis=0, keepdims=True), jnp.min(mn, axis=0, keepdims=True)

    def active_queries(lo, hi, cnt_lo):
        return (cnt_lo > kf) & ((hi - lo) != 1)

    def n_true(mask):
        return jnp.sum(jnp.where(mask, 1.0, 0.0))

    few = qpos < kc_score
    lo0 = jnp.where(few, KEY_LOWEST, _f32_to_key(grp_min))
    hi0 = _f32_to_key(jnp.maximum(2.0 * row_max, TINY_NORMAL))
    cnt_lo0 = jnp.where(few, (qpos + 1).astype(F32), count_ge(_key_to_f32(lo0)))
    cnt_hi0 = jnp.zeros((1, tq), F32)

    def a_step(it, st):
        lo, hi, cnt_lo, cnt_hi, same = st
        act = active_queries(lo, hi, cnt_lo)
        key_mid = lo + lax.shift_right_logical(hi - lo, jnp.ones_like(lo))
        val_mid = 0.5 * _key_to_f32(lo) + 0.5 * _key_to_f32(hi)
        val_key = jnp.minimum(jnp.maximum(_f32_to_key(val_mid), lo + 1), hi - 1)
        use_val = (it % KEY_MID_EVERY != KEY_MID_EVERY - 1) & (jnp.abs(val_mid) < jnp.inf)
        mid = jnp.where(use_val, val_key, key_mid)
        c = count_ge(_key_to_f32(mid))
        ge = c >= kf
        take_lo = act & ge
        take_hi = act & jnp.logical_not(ge)
        new_lo = jnp.where(take_lo, c, cnt_lo)
        new_hi = jnp.where(take_hi, c, cnt_hi)
        same = jnp.where((new_lo - new_hi) == (cnt_lo - cnt_hi), same + 1.0, 0.0)
        return jnp.where(take_lo, mid, lo), jnp.where(take_hi, mid, hi), new_lo, new_hi, same

    def n_wide(st):
        lo, hi, cnt_lo, cnt_hi, same = st
        wide = (active_queries(lo, hi, cnt_lo) & ((cnt_lo - cnt_hi) > BRACKET_SMALL)
                & (same < STAGNANT_PASSES))
        return n_true(wide)

    def a_body(carry):
        _, it, st = carry
        st = a_step(it, st)
        return n_wide(st), it + 1, st

    st = (lo0, hi0, cnt_lo0, cnt_hi0, jnp.zeros((1, tq), F32))
    st = lax.fori_loop(0, BLIND_PASSES, a_step, st)
    _, _, (lo, hi, cnt_lo, cnt_hi, _) = lax.while_loop(
        lambda c: c[0] > 0.0, a_body, (n_wide(st), jnp.int32(BLIND_PASSES), st))

    def b_body(st):
        _, lo, cnt_lo, cnt_gt, fin = st
        pend = (cnt_lo > kf) & (fin == 0.0)
        c_gt, nxt = next_above(_key_to_f32(lo))
        done = pend & (c_gt < kf)
        step = pend & jnp.logical_not(done)
        cnt_gt = jnp.where(done, c_gt, cnt_gt)
        fin = jnp.where(done, 1.0, fin)
        lo = jnp.where(step, _f32_to_key(nxt), lo)
        cnt_lo = jnp.where(step, c_gt, cnt_lo)
        return (n_true(step & (cnt_lo > kf)), lo, cnt_lo, cnt_gt, fin)

    zeros = jnp.zeros((1, tq), F32)
    _, lo_b, cnt_lo_b, cnt_gt, _ = lax.while_loop(
        lambda st: st[0] > 0.0, b_body, (n_true(cnt_lo > kf), lo, cnt_lo, zeros, zeros))

    t_lo = _key_to_f32(lo_b)
    need = jnp.where(cnt_lo_b > kf, kf - cnt_gt, jnp.inf)

    accT_ref[...] = jnp.zeros(accT_ref.shape, F32)
    tri = (lax.broadcasted_iota(I32, (kc_attn, kc_attn), 0)
           >= lax.broadcasted_iota(I32, (kc_attn, kc_attn), 1)).astype(BF16)

    n_att = n_keys // kc_attn

    def logits_stage(j, slot, ties_seen):
        off = pl.multiple_of(j * kc_attn, kc_attn)
        s = sT_ref[pl.ds(off, kc_attn), :]
        is_eq = s == t_lo
        rank = jnp.dot(tri, jnp.where(is_eq, 1.0, 0.0).astype(BF16), preferred_element_type=F32)
        sel = (s > t_lo) | (is_eq & (rank + ties_seen <= need))
        bias = jnp.where(sel, 0.0, MASK_VAL)
        cmax = []
        for hh in range(N_HEADS):
            pair = hh // 2
            kp = k_ref[0, pl.ds(off, kc_attn), pair * LANES:(pair + 1) * LANES]
            lg = jnp.dot(kp, qT_ref[0, hh * LANES:(hh + 1) * LANES, :], preferred_element_type=F32) + bias
            lg_ref[slot, hh] = lg
            cmax.append(jnp.max(lg, axis=0, keepdims=True))
        return jnp.concatenate(cmax, axis=0), ties_seen + rank[kc_attn - 1:kc_attn, :]

    def attn_chunk(j, carry):
        m_old, l_old, cmax, ties_seen = carry
        off = pl.multiple_of(j * kc_attn, kc_attn)
        slot = j % 2
        m_new = jnp.maximum(m_old, cmax)
        alpha = jnp.exp2(m_old - m_new)
        psum = []
        for hh in range(N_HEADS):
            p = jnp.exp2(lg_ref[slot, hh] - m_new[hh:hh + 1, :])
            psum.append(jnp.sum(p, axis=0, keepdims=True))
            hs = slice(hh * HEAD_DIM, (hh + 1) * HEAD_DIM)
            pv = jnp.dot(vT_ref[0, hs, pl.ds(off, kc_attn)], p.astype(BF16), preferred_element_type=F32)
            accT_ref[hs, :] = alpha[hh:hh + 1, :] * accT_ref[hs, :] + pv
        cmax_next, ties_seen = logits_stage(jnp.minimum(j + 1, n_att - 1), 1 - slot, ties_seen)
        return m_new, alpha * l_old + jnp.concatenate(psum, axis=0), cmax_next, ties_seen

    m0 = jnp.full((N_HEADS, tq), MASK_VAL, F32)
    l0 = jnp.zeros((N_HEADS, tq), F32)
    cmax0, ties0 = logits_stage(0, 0, jnp.zeros((1, tq), F32))
    _, lsum, _, _ = lax.fori_loop(0, n_att, attn_chunk, (m0, l0, cmax0, ties0))

    for hh in range(N_HEADS):
        hs = slice(hh * HEAD_DIM, (hh + 1) * HEAD_DIM)
        accT_ref[hs, :] = accT_ref[hs, :] / lsum[hh:hh + 1, :]
    o_ref[0] = accT_ref[...].T.astype(o_ref.dtype)


def _conv_kernel(x_ref, gmix_ref, wci_ref, cw_ref, cb_ref, lng_ref, lnb_ref, o_ref, ext_ref, sh_ref, y_ref):
    tm = x_ref.shape[1]
    i = pl.program_id(1)
    h = _rms_h(x_ref[0], gmix_ref[...])
    u = jnp.dot(h, wci_ref[...], preferred_element_type=F32)

    @pl.when(i == 0)
    def _():
        ext_ref[0:CONV_HALO, :] = jnp.zeros((CONV_HALO, CONV_CH), F32)

    ext_ref[CONV_HALO:CONV_HALO + tm, :] = u[:, :CONV_CH] * jax.nn.sigmoid(u[:, CONV_CH:])
    n_sh = tm + CONV_HALO - SUBLANES
    for b in range(1, SUBLANES):
        sh_ref[b - 1] = ext_ref[b:b + n_sh, :]

    base = CONV_HALO - (CONV_KERNEL - 1)

    def row_block(r, carry):
        r0 = pl.multiple_of(r * CONV_ROWS, CONV_ROWS)
        y = jnp.broadcast_to(cb_ref[...], (CONV_ROWS, CONV_CH))
        for j in range(CONV_KERNEL):
            a, b = divmod(base + j, SUBLANES)
            src = ext_ref if b == 0 else sh_ref.at[b - 1]
            y = y + cw_ref[j:j + 1, :] * src[pl.ds(r0 + a * SUBLANES, CONV_ROWS), :]
        y_ref[pl.ds(r0, CONV_ROWS), :] = y
        return carry

    lax.fori_loop(0, tm // CONV_ROWS, row_block, 0)
    y = y_ref[...]
    mu = jnp.mean(y, axis=-1, keepdims=True)
    yc = y - mu
    var = jnp.mean(yc * yc, axis=-1, keepdims=True)
    z = yc * lax.rsqrt(var + EPS) * lng_ref[...] + lnb_ref[...]
    o_ref[0] = (z * jax.nn.sigmoid(z)).astype(o_ref.dtype)
    ext_ref[0:CONV_HALO, :] = ext_ref[tm:tm + CONV_HALO, :]


def _mix_kernel(x_ref, attn_ref, conv_ref, p_ref, gmix_ref, wga_ref, wgc_ref, wao_ref, wco_ref, wout_ref,
                gffn_ref, wfg_ref, wfu_ref, wfd_ref, gple_ref, wpg_ref, wpp_ref, o_ref, *, ff_chunk):
    x = x_ref[...]
    h = _rms_h(x, gmix_ref[...])
    ga = jax.nn.sigmoid(jnp.dot(h, wga_ref[...], preferred_element_type=F32))
    gc = jax.nn.sigmoid(jnp.dot(h, wgc_ref[...], preferred_element_type=F32))
    merged = (ga * jnp.dot(attn_ref[...], wao_ref[...], preferred_element_type=F32)
              + gc * jnp.dot(conv_ref[...], wco_ref[...], preferred_element_type=F32))
    x = x + jnp.dot(merged.astype(BF16), wout_ref[...], preferred_element_type=F32)

    hf = _rms_h(x, gffn_ref[...])
    d_ff = wfg_ref.shape[1]
    ffn = jnp.zeros(x.shape, F32)
    for c in range(d_ff // ff_chunk):
        sl = slice(c * ff_chunk, (c + 1) * ff_chunk)
        g = jnp.dot(hf, wfg_ref[:, sl], preferred_element_type=F32)
        up = jnp.dot(hf, wfu_ref[:, sl], preferred_element_type=F32)
        act = (g * jax.nn.sigmoid(g) * up).astype(BF16)
        ffn = ffn + jnp.dot(act, wfd_ref[sl, :], preferred_element_type=F32)
    x = x + ffn

    hp = _rms_h(x, gple_ref[...])
    gate = jax.nn.sigmoid(jnp.dot(hp, wpg_ref[...], preferred_element_type=F32))
    emb = jnp.dot(p_ref[...].astype(BF16), wpp_ref[...], preferred_element_type=F32)
    o_ref[...] = x + gate * emb


def _const_spec(shape):
    nd = len(shape)
    return pl.BlockSpec(shape, lambda *_: (0,) * nd, pipeline_mode=pl.Buffered(1))


def _layer(x, p, g_mix, w_in, g_q, g_k, conv_w, conv_b, conv_ln_g, conv_ln_b, w_attn_o, w_conv_o, w_out,
           g_ffn, w_ffn_gate, w_ffn_up, w_ffn_down, g_ple, w_ple_gate, w_ple_proj):
    B, S, D = x.shape
    T = B * S
    topk = min(TOPK_MAX, S // 4)
    x2 = x.reshape(T, D)

    sizes = (ATTN_WIDTH, ATTN_WIDTH, ATTN_WIDTH, N_IDX_HEADS * IDX_DIM, IDX_DIM, N_IDX_HEADS,
             2 * CONV_CH, D, D)
    offs = [0]
    for s in sizes:
        offs.append(offs[-1] + s)
    wcols = [w_in[:, offs[i]:offs[i + 1]] for i in range(len(sizes))]
    wq, wk, wv, wqi, wki, wwi, wci, wga, wgc = wcols
    wqT = wq.T.astype(BF16)
    wk = wk.astype(BF16)
    wvT = wv.T.astype(BF16)
    wqiT = wqi.T.astype(BF16)
    wki2 = jnp.concatenate([wki, wki], axis=1).astype(BF16)
    wwiT = wwi.T.astype(BF16)
    gmix = g_mix.reshape(1, D)
    q_scale = (HEAD_DIM ** -0.5) * math.log2(math.e)
    gq = (jnp.tile(g_q, N_HEADS) * q_scale).reshape(ATTN_WIDTH, 1)
    gk = jnp.tile(g_k, N_HEADS).reshape(1, ATTN_WIDTH)
    head_id = jnp.arange(ATTN_WIDTH) // HEAD_DIM
    hd = (head_id[:, None] == head_id[None, :]).astype(BF16)

    tm = 512
    ns = S // tm
    row_tile = lambda i: (i, 0)
    seq_tile = lambda i: (i // ns, 0, i % ns)
    proj = pl.pallas_call(
        _proj_kernel,
        grid=(T // tm,),
        in_specs=[
            pl.BlockSpec((tm, D), row_tile),
            _const_spec((1, D)),
            _const_spec((ATTN_WIDTH, D)), _const_spec((D, ATTN_WIDTH)), _const_spec((ATTN_WIDTH, D)),
            _const_spec((ATTN_WIDTH, D)), _const_spec((D, LANES)), _const_spec((N_IDX_HEADS, D)),
            _const_spec((ATTN_WIDTH, 1)), _const_spec((1, ATTN_WIDTH)), _const_spec((ATTN_WIDTH, ATTN_WIDTH)),
        ],
        out_specs=[
            pl.BlockSpec((1, 2 * ATTN_WIDTH, tm), seq_tile),
            pl.BlockSpec((tm, ATTN_WIDTH), row_tile),
            pl.BlockSpec((1, ATTN_WIDTH, tm), seq_tile),
            pl.BlockSpec((1, 2 * ATTN_WIDTH, tm), seq_tile),
            pl.BlockSpec((tm, LANES), row_tile),
            pl.BlockSpec((1, N_IDX_HEADS, tm), seq_tile),
        ],
        out_shape=[
            jax.ShapeDtypeStruct((B, 2 * ATTN_WIDTH, S), BF16),
            jax.ShapeDtypeStruct((T, ATTN_WIDTH), BF16),
            jax.ShapeDtypeStruct((B, ATTN_WIDTH, S), BF16),
            jax.ShapeDtypeStruct((B, 2 * ATTN_WIDTH, S), BF16),
            jax.ShapeDtypeStruct((T, LANES), BF16),
            jax.ShapeDtypeStruct((B, N_IDX_HEADS, S), F32),
        ],
        compiler_params=pltpu.CompilerParams(dimension_semantics=("arbitrary",), vmem_limit_bytes=VMEM_LIMIT),
        name="dsa_proj",
    )
    qT, k, vT, qiT, ki2, wT = proj(x2, gmix, wqT, wk, wvT, wqiT, wki2, wwiT, gq, gk, hd)

    tq = 256
    kc_score = 256
    kc_attn = 256
    assert topk <= kc_score
    resident = functools.partial(pl.BlockSpec, pipeline_mode=pl.Buffered(1))
    dsa = pl.pallas_call(
        functools.partial(_dsa_kernel, topk=topk, kc_score=kc_score, kc_count=256, kc_attn=kc_attn),
        grid=(B, S // tq),
        in_specs=[
            pl.BlockSpec((1, 2 * ATTN_WIDTH, tq), lambda b, n: (b, 0, n)),
            pl.BlockSpec((1, 2 * ATTN_WIDTH, tq), lambda b, n: (b, 0, n)),
            pl.BlockSpec((1, N_IDX_HEADS, tq), lambda b, n: (b, 0, n)),
            resident((1, S, ATTN_WIDTH), lambda b, n: (b, 0, 0)),
            resident((1, S, LANES), lambda b, n: (b, 0, 0)),
            resident((1, ATTN_WIDTH, S), lambda b, n: (b, 0, 0)),
        ],
        out_specs=pl.BlockSpec((1, tq, ATTN_WIDTH), lambda b, n: (b, n, 0)),
        out_shape=jax.ShapeDtypeStruct((B, S, ATTN_WIDTH), BF16),
        scratch_shapes=[
            pltpu.VMEM((S, tq), F32),
            pltpu.VMEM((ATTN_WIDTH, tq), F32),
            pltpu.VMEM((2, N_HEADS, kc_attn, tq), F32),
        ],
        compiler_params=pltpu.CompilerParams(dimension_semantics=("arbitrary", "arbitrary"),
                                             vmem_limit_bytes=VMEM_LIMIT),
        name="dsa_attn",
    )
    attn = dsa(qT, qiT, wT, k.reshape(B, S, ATTN_WIDTH), ki2.reshape(B, S, LANES), vT)

    tmc = 512
    cw = jnp.pad(conv_w.reshape(CONV_KERNEL, CONV_CH), ((0, CONV_HALO - CONV_KERNEL), (0, 0)))
    conv = pl.pallas_call(
        _conv_kernel,
        grid=(B, S // tmc),
        in_specs=[
            pl.BlockSpec((1, tmc, D), lambda b, i: (b, i, 0)),
            _const_spec((1, D)),
            _const_spec((D, 2 * CONV_CH)),
            _const_spec((CONV_HALO, CONV_CH)),
            _const_spec((1, CONV_CH)), _const_spec((1, CONV_CH)), _const_spec((1, CONV_CH)),
        ],
        out_specs=pl.BlockSpec((1, tmc, CONV_CH), lambda b, i: (b, i, 0)),
        out_shape=jax.ShapeDtypeStruct((B, S, CONV_CH), BF16),
        scratch_shapes=[pltpu.VMEM((tmc + CONV_HALO, CONV_CH), F32),
                        pltpu.VMEM((SUBLANES - 1, tmc + CONV_HALO - SUBLANES, CONV_CH), F32),
                        pltpu.VMEM((tmc, CONV_CH), F32)],
        compiler_params=pltpu.CompilerParams(dimension_semantics=("arbitrary", "arbitrary"),
                                             vmem_limit_bytes=VMEM_LIMIT),
        name="conv_module",
    )(x, gmix, wci.astype(BF16), cw, conv_b.reshape(1, CONV_CH), conv_ln_g.reshape(1, CONV_CH),
      conv_ln_b.reshape(1, CONV_CH))

    tmd = 512
    d_ff = w_ffn_gate.shape[1]
    ple = p.shape[-1]
    ff_chunk = 256 if d_ff % 256 == 0 else d_ff
    out = pl.pallas_call(
        functools.partial(_mix_kernel, ff_chunk=ff_chunk),
        grid=(T // tmd,),
        in_specs=[
            pl.BlockSpec((tmd, D), row_tile),
            pl.BlockSpec((tmd, ATTN_WIDTH), row_tile),
            pl.BlockSpec((tmd, CONV_CH), row_tile),
            pl.BlockSpec((tmd, ple), row_tile),
            _const_spec((1, D)),
            _const_spec((D, D)), _const_spec((D, D)),
            _const_spec((ATTN_WIDTH, D)), _const_spec((CONV_CH, D)), _const_spec((D, D)),
            _const_spec((1, D)),
            _const_spec((D, d_ff)), _const_spec((D, d_ff)), _const_spec((d_ff, D)),
            _const_spec((1, D)),
            _const_spec((D, D)), _const_spec((ple, D)),
        ],
        out_specs=pl.BlockSpec((tmd, D), row_tile),
        out_shape=jax.ShapeDtypeStruct((T, D), F32),
        compiler_params=pltpu.CompilerParams(dimension_semantics=("arbitrary",), vmem_limit_bytes=VMEM_LIMIT),
        name="mix_ffn_ple",
    )(x2, attn.reshape(T, ATTN_WIDTH), conv.reshape(T, CONV_CH), p.reshape(T, ple),
      gmix, wga.astype(BF16), wgc.astype(BF16), w_attn_o.astype(BF16), w_conv_o.astype(BF16),
      w_out.astype(BF16), g_ffn.reshape(1, D), w_ffn_gate.astype(BF16), w_ffn_up.astype(BF16),
      w_ffn_down.astype(BF16), g_ple.reshape(1, D), w_ple_gate.astype(BF16), w_ple_proj.astype(BF16))
    return out.reshape(B, S, D)


def kernel(x, p, g_mix, w_in, g_q, g_k, conv_w, conv_b, conv_ln_g, conv_ln_b, w_attn_o, w_conv_o, w_out,
           g_ffn, w_ffn_gate, w_ffn_up, w_ffn_down, g_ple, w_ple_gate, w_ple_proj):
    depth = w_in.shape[0]
    for i in range(depth):
        x = _layer(x, p[i], g_mix[i], w_in[i], g_q[i], g_k[i], conv_w[i], conv_b[i], conv_ln_g[i],
                   conv_ln_b[i], w_attn_o[i], w_conv_o[i], w_out[i], g_ffn[i], w_ffn_gate[i],
                   w_ffn_up[i], w_ffn_down[i], g_ple[i], w_ple_gate[i], w_ple_proj[i])
    return x
```

```python
import functools
import math

import jax
import jax.numpy as jnp
from jax import lax
from jax.experimental import pallas as pl
from jax.experimental.pallas import tpu as pltpu

F32 = jnp.float32
BF16 = jnp.bfloat16
I32 = jnp.int32

N_HEADS = 8
HEAD_DIM = 64
ATTN_WIDTH = N_HEADS * HEAD_DIM
N_IDX_HEADS = 8
IDX_DIM = 64
TOPK_MAX = 256
CONV_CH = 512
CONV_KERNEL = 31
EPS = 1e-6

LANES = 128
SUBLANES = 8
CONV_HALO = 32
CONV_ROWS = 64
VMEM_LIMIT = 56 * 1024 * 1024

KEY_LOWEST = -2139095040
KEY_INF = 2139095040
MASK_VAL = -1e30
TINY_NORMAL = 1e-30
BRACKET_SMALL = 4.0
STAGNANT_PASSES = 2.0
BLIND_PASSES = 10
KEY_MID_EVERY = 4

NT_DIMS = (((1,), (1,)), ((), ()))


def _key_to_f32(k):
    bits = k ^ ((k >> 31) & 0x7FFFFFFF)
    return lax.bitcast_convert_type(bits, F32)


def _f32_to_key(x):
    bits = lax.bitcast_convert_type(x, I32)
    return bits ^ ((bits >> 31) & 0x7FFFFFFF)


def _rms_h(x, g):
    ms = jnp.mean(x * x, axis=-1, keepdims=True)
    return (x * lax.rsqrt(ms + EPS) * g).astype(BF16)


def _split_bf16(a):
    hi = a.astype(BF16)
    lo = (a - hi.astype(F32)).astype(BF16)
    return hi, lo


def _proj_kernel(x_ref, gmix_ref, wqT_ref, wk_ref, wvT_ref, wqiT_ref, wki_ref, wwiT_ref,
                 gq_ref, gk_ref, hd_ref,
                 qT_out, k_out, vT_out, qiT_out, ki_out, wT_out):
    tm = x_ref.shape[0]
    h = _rms_h(x_ref[...], gmix_ref[...])
    hd = hd_ref[...]

    row = lax.broadcasted_iota(I32, (LANES, tm), 0)
    top_half = row < HEAD_DIM

    def expand_heads(yT, out_ref):
        for pair in range(N_HEADS // 2):
            blk = yT[pair * LANES:(pair + 1) * LANES, :]
            out_ref[0, (2 * pair) * LANES:(2 * pair + 1) * LANES, :] = jnp.where(top_half, blk, 0.0).astype(BF16)
            out_ref[0, (2 * pair + 1) * LANES:(2 * pair + 2) * LANES, :] = jnp.where(top_half, 0.0, blk).astype(BF16)

    qT = lax.dot_general(wqT_ref[...], h, NT_DIMS, preferred_element_type=F32)
    hi, lo = _split_bf16(qT * qT)
    ssq = jnp.dot(hd, hi, preferred_element_type=F32) + jnp.dot(hd, lo, preferred_element_type=F32)
    expand_heads(qT * lax.rsqrt(ssq * (1.0 / HEAD_DIM) + EPS) * gq_ref[...], qT_out)

    k = jnp.dot(h, wk_ref[...], preferred_element_type=F32)
    hi, lo = _split_bf16(k * k)
    ssk = jnp.dot(hi, hd, preferred_element_type=F32) + jnp.dot(lo, hd, preferred_element_type=F32)
    k_out[...] = (k * lax.rsqrt(ssk * (1.0 / HEAD_DIM) + EPS) * gk_ref[...]).astype(BF16)

    vT_out[0] = lax.dot_general(wvT_ref[...], h, NT_DIMS, preferred_element_type=F32).astype(BF16)

    qiT = lax.dot_general(wqiT_ref[...], h, NT_DIMS, preferred_element_type=F32) * (IDX_DIM ** -0.5)
    expand_heads(qiT, qiT_out)

    ki_out[...] = jnp.dot(h, wki_ref[...], preferred_element_type=F32).astype(BF16)
    wT_out[0] = (lax.dot_general(wwiT_ref[...], h, NT_DIMS, preferred_element_type=F32)
                 * (N_IDX_HEADS ** -0.5))


def _dsa_kernel(qT_ref, qiT_ref, wT_ref, k_ref, ki_ref, vT_ref, o_ref,
                sT_ref, accT_ref, lg_ref, *, topk, kc_score, kc_count, kc_attn):
    tq = qT_ref.shape[2]
    n = pl.program_id(1)
    t0 = n * tq
    n_keys = t0 + tq
    qpos = t0 + lax.broadcasted_iota(I32, (1, tq), 1)
    kf = float(topk)

    def score_rows(off, rows, masked):
        ki = ki_ref[0, pl.ds(off, rows), :]
        acc = jnp.zeros((rows, tq), F32)
        for hh in range(N_IDX_HEADS):
            z = jnp.dot(ki, qiT_ref[0, hh * LANES:(hh + 1) * LANES, :], preferred_element_type=F32)
            acc = acc + jnp.maximum(z, 0.0) * wT_ref[0, hh:hh + 1, :]
        if masked:
            kpos = off + lax.broadcasted_iota(I32, (rows, tq), 0)
            acc = jnp.where(kpos <= qpos, acc, -jnp.inf)
        sT_ref[pl.ds(off, rows), :] = acc
        return acc

    def big_chunk(j, g):
        acc = score_rows(pl.multiple_of(j * 2 * kc_score, 2 * kc_score), 2 * kc_score, False)
        return jnp.maximum(g, jnp.maximum(acc[:kc_score], acc[kc_score:]))

    n_big = t0 // (2 * kc_score)
    g = lax.fori_loop(0, n_big, big_chunk, jnp.full((kc_score, tq), -jnp.inf, F32))
    rem_off = pl.multiple_of(n_big * 2 * kc_score, kc_score)
    g = lax.cond(rem_off < t0, lambda g: jnp.maximum(g, score_rows(rem_off, kc_score, False)), lambda g: g, g)
    for d in range(tq // kc_score):
        g = jnp.maximum(g, score_rows(pl.multiple_of(t0 + d * kc_score, kc_score), kc_score, True))
    grp_min = jnp.min(g, axis=0, keepdims=True)
    row_max = jnp.max(g, axis=0, keepdims=True)

    n_cnt = n_keys // kc_count

    def sweep(init, body_fn):
        def chunk(acc, off):
            return body_fn(acc, sT_ref[pl.ds(pl.multiple_of(off, kc_count), kc_count), :])

        def body(j, acc):
            return chunk(chunk(acc, 2 * j * kc_count), (2 * j + 1) * kc_count)

        acc = lax.fori_loop(0, n_cnt // 2, body, init)
        return lax.cond(n_cnt % 2 == 1, lambda a: chunk(a, (n_cnt - 1) * kc_count), lambda a: a, acc)

    def fold(x, op):
        return op(x.reshape(kc_count // SUBLANES, SUBLANES, tq), axis=0)

    def count_ge(t_row):
        acc = sweep(jnp.zeros((SUBLANES, tq), F32),
                    lambda a, s: a + fold(jnp.where(s >= t_row, 1.0, 0.0), jnp.sum))
        return jnp.sum(acc, axis=0, keepdims=True)

    def next_above(t_row):
        def body(acc, s):
            c, mn = acc
            gt = s > t_row
            return (c + fold(jnp.where(gt, 1.0, 0.0), jnp.sum),
                    jnp.minimum(mn, fold(jnp.where(gt, s, jnp.inf), jnp.min)))
        c, mn = sweep((jnp.zeros((SUBLANES, tq), F32), jnp.full((SUBLANES, tq), jnp.inf, F32)), body)
        return jnp.sum(c, axis=0, keepdims=True), jnp.min(mn, axis=0, keepdims=True)

    def active_queries(lo, hi, cnt_lo):
        return (cnt_lo > kf) & ((hi - lo) != 1)

    def n_true(mask):
        return jnp.sum(jnp.where(mask, 1.0, 0.0))

    few = qpos < kc_score
    lo0 = jnp.where(few, KEY_LOWEST, _f32_to_key(grp_min))
    hi0 = _f32_to_key(jnp.maximum(2.0 * row_max, TINY_NORMAL))
    cnt_lo0 = jnp.where(few, (qpos + 1).astype(F32), count_ge(_key_to_f32(lo0)))
    cnt_hi0 = jnp.zeros((1, tq), F32)

    def a_step(it, st):
        lo, hi, cnt_lo, cnt_hi, same = st
        act = active_queries(lo, hi, cnt_lo)
        key_mid = lo + lax.shift_right_logical(hi - lo, jnp.ones_like(lo))
        val_mid = 0.5 * _key_to_f32(lo) + 0.5 * _key_to_f32(hi)
        val_key = jnp.minimum(jnp.maximum(_f32_to_key(val_mid), lo + 1), hi - 1)
        use_val = (it % KEY_MID_EVERY != KEY_MID_EVERY - 1) & (jnp.abs(val_mid) < jnp.inf)
        mid = jnp.where(use_val, val_key, key_mid)
        c = count_ge(_key_to_f32(mid))
        ge = c >= kf
        take_lo = act & ge
        take_hi = act & jnp.logical_not(ge)
        new_lo = jnp.where(take_lo, c, cnt_lo)
        new_hi = jnp.where(take_hi, c, cnt_hi)
        same = jnp.where((new_lo - new_hi) == (cnt_lo - cnt_hi), same + 1.0, 0.0)
        return jnp.where(take_lo, mid, lo), jnp.where(take_hi, mid, hi), new_lo, new_hi, same

    def n_wide(st):
        lo, hi, cnt_lo, cnt_hi, same = st
        wide = (active_queries(lo, hi, cnt_lo) & ((cnt_lo - cnt_hi) > BRACKET_SMALL)
                & (same < STAGNANT_PASSES))
        return n_true(wide)

    def a_body(carry):
        _, it, st = carry
        st = a_step(it, st)
        return n_wide(st), it + 1, st

    st = (lo0, hi0, cnt_lo0, cnt_hi0, jnp.zeros((1, tq), F32))
    st = lax.fori_loop(0, BLIND_PASSES, a_step, st)
    _, _, (lo, hi, cnt_lo, cnt_hi, _) = lax.while_loop(
        lambda c: c[0] > 0.0, a_body, (n_wide(st), jnp.int32(BLIND_PASSES), st))

    def b_body(st):
        _, lo, cnt_lo, cnt_gt, fin = st
        pend = (cnt_lo > kf) & (fin == 0.0)
        c_gt, nxt = next_above(_key_to_f32(lo))
        done = pend & (c_gt < kf)
        step = pend & jnp.logical_not(done)
        cnt_gt = jnp.where(done, c_gt, cnt_gt)
        fin = jnp.where(done, 1.0, fin)
        lo = jnp.where(step, _f32_to_key(nxt), lo)
        cnt_lo = jnp.where(step, c_gt, cnt_lo)
        return (n_true(step & (cnt_lo > kf)), lo, cnt_lo, cnt_gt, fin)

    zeros = jnp.zeros((1, tq), F32)
    _, lo_b, cnt_lo_b, cnt_gt, _ = lax.while_loop(
        lambda st: st[0] > 0.0, b_body, (n_true(cnt_lo > kf), lo, cnt_lo, zeros, zeros))

    t_lo = _key_to_f32(lo_b)
    need = jnp.where(cnt_lo_b > kf, kf - cnt_gt, jnp.inf)

    accT_ref[...] = jnp.zeros(accT_ref.shape, F32)
    tri = (lax.broadcasted_iota(I32, (kc_attn, kc_attn), 0)
           >= lax.broadcasted_iota(I32, (kc_attn, kc_attn), 1)).astype(BF16)

    n_att = n_keys // kc_attn

    def logits_stage(j, slot, ties_seen):
        off = pl.multiple_of(j * kc_attn, kc_attn)
        s = sT_ref[pl.ds(off, kc_attn), :]
        is_eq = s == t_lo
        rank = jnp.dot(tri, jnp.where(is_eq, 1.0, 0.0).astype(BF16), preferred_element_type=F32)
        sel = (s > t_lo) | (is_eq & (rank + ties_seen <= need))
        bias = jnp.where(sel, 0.0, MASK_VAL)
        cmax = []
        for hh in range(N_HEADS):
            pair = hh // 2
            kp = k_ref[0, pl.ds(off, kc_attn), pair * LANES:(pair + 1) * LANES]
            lg = jnp.dot(kp, qT_ref[0, hh * LANES:(hh + 1) * LANES, :], preferred_element_type=F32) + bias
            lg_ref[slot, hh] = lg
            cmax.append(jnp.max(lg, axis=0, keepdims=True))
        return jnp.concatenate(cmax, axis=0), ties_seen + rank[kc_attn - 1:kc_attn, :]

    def softmax_stage(j, m_old, l_old, cmax):
        off = pl.multiple_of(j * kc_attn, kc_attn)
        slot = j % 2
        m_new = jnp.maximum(m_old, cmax)
        alpha = jnp.exp2(m_old - m_new)
        psum = []
        for hh in range(N_HEADS):
            p = jnp.exp2(lg_ref[slot, hh] - m_new[hh:hh + 1, :])
            psum.append(jnp.sum(p, axis=0, keepdims=True))
            hs = slice(hh * HEAD_DIM, (hh + 1) * HEAD_DIM)
            pv = jnp.dot(vT_ref[0, hs, pl.ds(off, kc_attn)], p.astype(BF16), preferred_element_type=F32)
            accT_ref[hs, :] = alpha[hh:hh + 1, :] * accT_ref[hs, :] + pv
        return m_new, alpha * l_old + jnp.concatenate(psum, axis=0)

    def attn_chunk(j, carry):
        m_old, l_old, cmax, ties_seen = carry
        m_new, l_new = softmax_stage(j, m_old, l_old, cmax)
        cmax_next, ties_seen = logits_stage(j + 1, 1 - j % 2, ties_seen)
        return m_new, l_new, cmax_next, ties_seen

    m0 = jnp.full((N_HEADS, tq), MASK_VAL, F32)
    l0 = jnp.zeros((N_HEADS, tq), F32)
    cmax0, ties0 = logits_stage(0, 0, jnp.zeros((1, tq), F32))
    m_last, l_last, cmax_last, _ = lax.fori_loop(0, n_att - 1, attn_chunk, (m0, l0, cmax0, ties0))
    _, lsum = softmax_stage(n_att - 1, m_last, l_last, cmax_last)

    for hh in range(N_HEADS):
        hs = slice(hh * HEAD_DIM, (hh + 1) * HEAD_DIM)
        accT_ref[hs, :] = accT_ref[hs, :] / lsum[hh:hh + 1, :]
    o_ref[0] = accT_ref[...].T.astype(o_ref.dtype)


def _conv_kernel(x_ref, gmix_ref, wci_ref, cw_ref, cb_ref, lng_ref, lnb_ref, o_ref, ext_ref, sh_ref, y_ref):
    tm = x_ref.shape[1]
    i = pl.program_id(1)
    h = _rms_h(x_ref[0], gmix_ref[...])
    u = jnp.dot(h, wci_ref[...], preferred_element_type=F32)

    @pl.when(i == 0)
    def _():
        ext_ref[0:CONV_HALO, :] = jnp.zeros((CONV_HALO, CONV_CH), F32)

    ext_ref[CONV_HALO:CONV_HALO + tm, :] = u[:, :CONV_CH] * jax.nn.sigmoid(u[:, CONV_CH:])
    n_sh = tm + CONV_HALO - SUBLANES
    for b in range(1, SUBLANES):
        sh_ref[b - 1] = ext_ref[b:b + n_sh, :]

    base = CONV_HALO - (CONV_KERNEL - 1)

    def row_block(r, carry):
        r0 = pl.multiple_of(r * CONV_ROWS, CONV_ROWS)
        y = jnp.broadcast_to(cb_ref[...], (CONV_ROWS, CONV_CH))
        for j in range(CONV_KERNEL):
            a, b = divmod(base + j, SUBLANES)
            src = ext_ref if b == 0 else sh_ref.at[b - 1]
            y = y + cw_ref[j:j + 1, :] * src[pl.ds(r0 + a * SUBLANES, CONV_ROWS), :]
        y_ref[pl.ds(r0, CONV_ROWS), :] = y
        return carry

    lax.fori_loop(0, tm // CONV_ROWS, row_block, 0)
    y = y_ref[...]
    mu = jnp.mean(y, axis=-1, keepdims=True)
    yc = y - mu
    var = jnp.mean(yc * yc, axis=-1, keepdims=True)
    z = yc * lax.rsqrt(var + EPS) * lng_ref[...] + lnb_ref[...]
    o_ref[0] = (z * jax.nn.sigmoid(z)).astype(o_ref.dtype)
    ext_ref[0:CONV_HALO, :] = ext_ref[tm:tm + CONV_HALO, :]


def _mix_kernel(x_ref, attn_ref, conv_ref, p_ref, gmix_ref, wga_ref, wgc_ref, wao_ref, wco_ref, wout_ref,
                gffn_ref, wfg_ref, wfu_ref, wfd_ref, gple_ref, wpg_ref, wpp_ref, o_ref, *, ff_chunk):
    x = x_ref[...]
    h = _rms_h(x, gmix_ref[...])
    ga = jax.nn.sigmoid(jnp.dot(h, wga_ref[...], preferred_element_type=F32))
    gc = jax.nn.sigmoid(jnp.dot(h, wgc_ref[...], preferred_element_type=F32))
    merged = (ga * jnp.dot(attn_ref[...], wao_ref[...], preferred_element_type=F32)
              + gc * jnp.dot(conv_ref[...], wco_ref[...], preferred_element_type=F32))
    x = x + jnp.dot(merged.astype(BF16), wout_ref[...], preferred_element_type=F32)

    hf = _rms_h(x, gffn_ref[...])
    d_ff = wfg_ref.shape[1]
    ffn = jnp.zeros(x.shape, F32)
    for c in range(d_ff // ff_chunk):
        sl = slice(c * ff_chunk, (c + 1) * ff_chunk)
        g = jnp.dot(hf, wfg_ref[:, sl], preferred_element_type=F32)
        up = jnp.dot(hf, wfu_ref[:, sl], preferred_element_type=F32)
        act = (g * jax.nn.sigmoid(g) * up).astype(BF16)
        ffn = ffn + jnp.dot(act, wfd_ref[sl, :], preferred_element_type=F32)
    x = x + ffn

    hp = _rms_h(x, gple_ref[...])
    gate = jax.nn.sigmoid(jnp.dot(hp, wpg_ref[...], preferred_element_type=F32))
    emb = jnp.dot(p_ref[...].astype(BF16), wpp_ref[...], preferred_element_type=F32)
    o_ref[...] = x + gate * emb


def _const_spec(shape):
    nd = len(shape)
    return pl.BlockSpec(shape, lambda *_: (0,) * nd, pipeline_mode=pl.Buffered(1))


def _layer(x, p, g_mix, w_in, g_q, g_k, conv_w, conv_b, conv_ln_g, conv_ln_b, w_attn_o, w_conv_o, w_out,
           g_ffn, w_ffn_gate, w_ffn_up, w_ffn_down, g_ple, w_ple_gate, w_ple_proj):
    B, S, D = x.shape
    T = B * S
    topk = min(TOPK_MAX, S // 4)
    x2 = x.reshape(T, D)

    sizes = (ATTN_WIDTH, ATTN_WIDTH, ATTN_WIDTH, N_IDX_HEADS * IDX_DIM, IDX_DIM, N_IDX_HEADS,
             2 * CONV_CH, D, D)
    offs = [0]
    for s in sizes:
        offs.append(offs[-1] + s)
    wcols = [w_in[:, offs[i]:offs[i + 1]] for i in range(len(sizes))]
    wq, wk, wv, wqi, wki, wwi, wci, wga, wgc = wcols
    wqT = wq.T.astype(BF16)
    wk = wk.astype(BF16)
    wvT = wv.T.astype(BF16)
    wqiT = wqi.T.astype(BF16)
    wki2 = jnp.concatenate([wki, wki], axis=1).astype(BF16)
    wwiT = wwi.T.astype(BF16)
    gmix = g_mix.reshape(1, D)
    q_scale = (HEAD_DIM ** -0.5) * math.log2(math.e)
    gq = (jnp.tile(g_q, N_HEADS) * q_scale).reshape(ATTN_WIDTH, 1)
    gk = jnp.tile(g_k, N_HEADS).reshape(1, ATTN_WIDTH)
    head_id = jnp.arange(ATTN_WIDTH) // HEAD_DIM
    hd = (head_id[:, None] == head_id[None, :]).astype(BF16)

    tm = 512
    ns = S // tm
    row_tile = lambda i: (i, 0)
    seq_tile = lambda i: (i // ns, 0, i % ns)
    proj = pl.pallas_call(
        _proj_kernel,
        grid=(T // tm,),
        in_specs=[
            pl.BlockSpec((tm, D), row_tile),
            _const_spec((1, D)),
            _const_spec((ATTN_WIDTH, D)), _const_spec((D, ATTN_WIDTH)), _const_spec((ATTN_WIDTH, D)),
            _const_spec((ATTN_WIDTH, D)), _const_spec((D, LANES)), _const_spec((N_IDX_HEADS, D)),
            _const_spec((ATTN_WIDTH, 1)), _const_spec((1, ATTN_WIDTH)), _const_spec((ATTN_WIDTH, ATTN_WIDTH)),
        ],
        out_specs=[
            pl.BlockSpec((1, 2 * ATTN_WIDTH, tm), seq_tile),
            pl.BlockSpec((tm, ATTN_WIDTH), row_tile),
            pl.BlockSpec((1, ATTN_WIDTH, tm), seq_tile),
            pl.BlockSpec((1, 2 * ATTN_WIDTH, tm), seq_tile),
            pl.BlockSpec((tm, LANES), row_tile),
            pl.BlockSpec((1, N_IDX_HEADS, tm), seq_tile),
        ],
        out_shape=[
            jax.ShapeDtypeStruct((B, 2 * ATTN_WIDTH, S), BF16),
            jax.ShapeDtypeStruct((T, ATTN_WIDTH), BF16),
            jax.ShapeDtypeStruct((B, ATTN_WIDTH, S), BF16),
            jax.ShapeDtypeStruct((B, 2 * ATTN_WIDTH, S), BF16),
            jax.ShapeDtypeStruct((T, LANES), BF16),
            jax.ShapeDtypeStruct((B, N_IDX_HEADS, S), F32),
        ],
        compiler_params=pltpu.CompilerParams(dimension_semantics=("arbitrary",), vmem_limit_bytes=VMEM_LIMIT),
        name="dsa_proj",
    )
    qT, k, vT, qiT, ki2, wT = proj(x2, gmix, wqT, wk, wvT, wqiT, wki2, wwiT, gq, gk, hd)

    tq = 256
    kc_score = 256
    kc_attn = 256
    assert topk <= kc_score
    resident = functools.partial(pl.BlockSpec, pipeline_mode=pl.Buffered(1))
    dsa = pl.pallas_call(
        functools.partial(_dsa_kernel, topk=topk, kc_score=kc_score, kc_count=256, kc_attn=kc_attn),
        grid=(B, S // tq),
        in_specs=[
            pl.BlockSpec((1, 2 * ATTN_WIDTH, tq), lambda b, n: (b, 0, n)),
            pl.BlockSpec((1, 2 * ATTN_WIDTH, tq), lambda b, n: (b, 0, n)),
            pl.BlockSpec((1, N_IDX_HEADS, tq), lambda b, n: (b, 0, n)),
            resident((1, S, ATTN_WIDTH), lambda b, n: (b, 0, 0)),
            resident((1, S, LANES), lambda b, n: (b, 0, 0)),
            resident((1, ATTN_WIDTH, S), lambda b, n: (b, 0, 0)),
        ],
        out_specs=pl.BlockSpec((1, tq, ATTN_WIDTH), lambda b, n: (b, n, 0)),
        out_shape=jax.ShapeDtypeStruct((B, S, ATTN_WIDTH), BF16),
        scratch_shapes=[
            pltpu.VMEM((S, tq), F32),
            pltpu.VMEM((ATTN_WIDTH, tq), F32),
            pltpu.VMEM((2, N_HEADS, kc_attn, tq), F32),
        ],
        compiler_params=pltpu.CompilerParams(dimension_semantics=("arbitrary", "arbitrary"),
                                             vmem_limit_bytes=VMEM_LIMIT),
        name="dsa_attn",
    )
    attn = dsa(qT, qiT, wT, k.reshape(B, S, ATTN_WIDTH), ki2.reshape(B, S, LANES), vT)

    tmc = 512
    cw = jnp.pad(conv_w.reshape(CONV_KERNEL, CONV_CH), ((0, CONV_HALO - CONV_KERNEL), (0, 0)))
    conv = pl.pallas_call(
        _conv_kernel,
        grid=(B, S // tmc),
        in_specs=[
            pl.BlockSpec((1, tmc, D), lambda b, i: (b, i, 0)),
            _const_spec((1, D)),
            _const_spec((D, 2 * CONV_CH)),
            _const_spec((CONV_HALO, CONV_CH)),
            _const_spec((1, CONV_CH)), _const_spec((1, CONV_CH)), _const_spec((1, CONV_CH)),
        ],
        out_specs=pl.BlockSpec((1, tmc, CONV_CH), lambda b, i: (b, i, 0)),
        out_shape=jax.ShapeDtypeStruct((B, S, CONV_CH), BF16),
        scratch_shapes=[pltpu.VMEM((tmc + CONV_HALO, CONV_CH), F32),
                        pltpu.VMEM((SUBLANES - 1, tmc + CONV_HALO - SUBLANES, CONV_CH), F32),
                        pltpu.VMEM((tmc, CONV_CH), F32)],
        compiler_params=pltpu.CompilerParams(dimension_semantics=("arbitrary", "arbitrary"),
                                             vmem_limit_bytes=VMEM_LIMIT),
        name="conv_module",
    )(x, gmix, wci.astype(BF16), cw, conv_b.reshape(1, CONV_CH), conv_ln_g.reshape(1, CONV_CH),
      conv_ln_b.reshape(1, CONV_CH))

    tmd = 512
    d_ff = w_ffn_gate.shape[1]
    ple = p.shape[-1]
    ff_chunk = 256 if d_ff % 256 == 0 else d_ff
    out = pl.pallas_call(
        functools.partial(_mix_kernel, ff_chunk=ff_chunk),
        grid=(T // tmd,),
        in_specs=[
            pl.BlockSpec((tmd, D), row_tile),
            pl.BlockSpec((tmd, ATTN_WIDTH), row_tile),
            pl.BlockSpec((tmd, CONV_CH), row_tile),
            pl.BlockSpec((tmd, ple), row_tile),
            _const_spec((1, D)),
            _const_spec((D, D)), _const_spec((D, D)),
            _const_spec((ATTN_WIDTH, D)), _const_spec((CONV_CH, D)), _const_spec((D, D)),
            _const_spec((1, D)),
            _const_spec((D, d_ff)), _const_spec((D, d_ff)), _const_spec((d_ff, D)),
            _const_spec((1, D)),
            _const_spec((D, D)), _const_spec((ple, D)),
        ],
        out_specs=pl.BlockSpec((tmd, D), row_tile),
        out_shape=jax.ShapeDtypeStruct((T, D), F32),
        compiler_params=pltpu.CompilerParams(dimension_semantics=("arbitrary",), vmem_limit_bytes=VMEM_LIMIT),
        name="mix_ffn_ple",
    )(x2, attn.reshape(T, ATTN_WIDTH), conv.reshape(T, CONV_CH), p.reshape(T, ple),
      gmix, wga.astype(BF16), wgc.astype(BF16), w_attn_o.astype(BF16), w_conv_o.astype(BF16),
      w_out.astype(BF16), g_ffn.reshape(1, D), w_ffn_gate.astype(BF16), w_ffn_up.astype(BF16),
      w_ffn_down.astype(BF16), g_ple.reshape(1, D), w_ple_gate.astype(BF16), w_ple_proj.astype(BF16))
    return out.reshape(B, S, D)


def kernel(x, p, g_mix, w_in, g_q, g_k, conv_w, conv_b, conv_ln_g, conv_ln_b, w_attn_o, w_conv_o, w_out,
           g_ffn, w_ffn_gate, w_ffn_up, w_ffn_down, g_ple, w_ple_gate, w_ple_proj):
    depth = w_in.shape[0]
    for i in range(depth):
        x = _layer(x, p[i], g_mix[i], w_in[i], g_q[i], g_k[i], conv_w[i], conv_b[i], conv_ln_g[i],
                   conv_ln_b[i], w_attn_o[i], w_conv_o[i], w_out[i], g_ffn[i], w_ffn_gate[i],
                   w_ffn_up[i], w_ffn_down[i], g_ple[i], w_ple_gate[i], w_ple_proj[i])
    return x
```

```python
import functools
import math

import jax
import jax.numpy as jnp
from jax import lax
from jax.experimental import pallas as pl
from jax.experimental.pallas import tpu as pltpu

F32 = jnp.float32
BF16 = jnp.bfloat16
I32 = jnp.int32

N_HEADS = 8
HEAD_DIM = 64
ATTN_WIDTH = N_HEADS * HEAD_DIM
N_IDX_HEADS = 8
IDX_DIM = 64
TOPK_MAX = 256
CONV_CH = 512
CONV_KERNEL = 31
EPS = 1e-6

LANES = 128
SUBLANES = 8
CONV_HALO = 32
CONV_ROWS = 64
VMEM_LIMIT = 56 * 1024 * 1024

KEY_LOWEST = -2139095040
KEY_INF = 2139095040
MASK_VAL = -1e30
TINY_NORMAL = 1e-30
BRACKET_SMALL = 4.0
STAGNANT_PASSES = 2.0
BLIND_PASSES = 10
KEY_MID_EVERY = 4

NT_DIMS = (((1,), (1,)), ((), ()))


def _key_to_f32(k):
    bits = k ^ ((k >> 31) & 0x7FFFFFFF)
    return lax.bitcast_convert_type(bits, F32)


def _f32_to_key(x):
    bits = lax.bitcast_convert_type(x, I32)
    return bits ^ ((bits >> 31) & 0x7FFFFFFF)


def _rms_h(x, g):
    ms = jnp.mean(x * x, axis=-1, keepdims=True)
    return (x * lax.rsqrt(ms + EPS) * g).astype(BF16)


def _split_bf16(a):
    hi = a.astype(BF16)
    lo = (a - hi.astype(F32)).astype(BF16)
    return hi, lo


def _proj_kernel(x_ref, gmix_ref, wqT_ref, wk_ref, wvT_ref, wqiT_ref, wki_ref, wwiT_ref,
                 gq_ref, gk_ref, hd_ref,
                 qT_out, k_out, vT_out, qiT_out, ki_out, wT_out):
    tm = x_ref.shape[0]
    h = _rms_h(x_ref[...], gmix_ref[...])
    hd = hd_ref[...]

    row = lax.broadcasted_iota(I32, (LANES, tm), 0)
    top_half = row < HEAD_DIM

    def expand_heads(yT, out_ref):
        for pair in range(N_HEADS // 2):
            blk = yT[pair * LANES:(pair + 1) * LANES, :]
            out_ref[0, (2 * pair) * LANES:(2 * pair + 1) * LANES, :] = jnp.where(top_half, blk, 0.0).astype(BF16)
            out_ref[0, (2 * pair + 1) * LANES:(2 * pair + 2) * LANES, :] = jnp.where(top_half, 0.0, blk).astype(BF16)

    qT = lax.dot_general(wqT_ref[...], h, NT_DIMS, preferred_element_type=F32)
    hi, lo = _split_bf16(qT * qT)
    ssq = jnp.dot(hd, hi, preferred_element_type=F32) + jnp.dot(hd, lo, preferred_element_type=F32)
    expand_heads(qT * lax.rsqrt(ssq * (1.0 / HEAD_DIM) + EPS) * gq_ref[...], qT_out)

    k = jnp.dot(h, wk_ref[...], preferred_element_type=F32)
    hi, lo = _split_bf16(k * k)
    ssk = jnp.dot(hi, hd, preferred_element_type=F32) + jnp.dot(lo, hd, preferred_element_type=F32)
    k_out[...] = (k * lax.rsqrt(ssk * (1.0 / HEAD_DIM) + EPS) * gk_ref[...]).astype(BF16)

    vT_out[0] = lax.dot_general(wvT_ref[...], h, NT_DIMS, preferred_element_type=F32).astype(BF16)

    qiT = lax.dot_general(wqiT_ref[...], h, NT_DIMS, preferred_element_type=F32) * (IDX_DIM ** -0.5)
    expand_heads(qiT, qiT_out)

    ki_out[...] = jnp.dot(h, wki_ref[...], preferred_element_type=F32).astype(BF16)
    wT_out[0] = (lax.dot_general(wwiT_ref[...], h, NT_DIMS, preferred_element_type=F32)
                 * (N_IDX_HEADS ** -0.5))


def _dsa_kernel(qT_ref, qiT_ref, wT_ref, k_ref, ki_ref, vT_ref, o_ref,
                sT_ref, accT_ref, lga_ref, lgb_ref, *, topk, kc_score, kc_count, kc_attn):
    tq = qT_ref.shape[2]
    n = pl.program_id(1)
    t0 = n * tq
    n_keys = t0 + tq
    qpos = t0 + lax.broadcasted_iota(I32, (1, tq), 1)
    kf = float(topk)

    def score_rows(off, rows, masked):
        ki = ki_ref[0, pl.ds(off, rows), :]
        acc = jnp.zeros((rows, tq), F32)
        for hh in range(N_IDX_HEADS):
            z = jnp.dot(ki, qiT_ref[0, hh * LANES:(hh + 1) * LANES, :], preferred_element_type=F32)
            acc = acc + jnp.maximum(z, 0.0) * wT_ref[0, hh:hh + 1, :]
        if masked:
            kpos = off + lax.broadcasted_iota(I32, (rows, tq), 0)
            acc = jnp.where(kpos <= qpos, acc, -jnp.inf)
        sT_ref[pl.ds(off, rows), :] = acc
        return acc

    def big_chunk(j, g):
        acc = score_rows(pl.multiple_of(j * 2 * kc_score, 2 * kc_score), 2 * kc_score, False)
        return jnp.maximum(g, jnp.maximum(acc[:kc_score], acc[kc_score:]))

    n_big = t0 // (2 * kc_score)
    g = lax.fori_loop(0, n_big, big_chunk, jnp.full((kc_score, tq), -jnp.inf, F32))
    rem_off = pl.multiple_of(n_big * 2 * kc_score, kc_score)
    g = lax.cond(rem_off < t0, lambda g: jnp.maximum(g, score_rows(rem_off, kc_score, False)), lambda g: g, g)
    for d in range(tq // kc_score):
        g = jnp.maximum(g, score_rows(pl.multiple_of(t0 + d * kc_score, kc_score), kc_score, True))
    grp_min = jnp.min(g, axis=0, keepdims=True)
    row_max = jnp.max(g, axis=0, keepdims=True)

    n_cnt = n_keys // kc_count

    def sweep(init, body_fn):
        def chunk(acc, off):
            return body_fn(acc, sT_ref[pl.ds(pl.multiple_of(off, kc_count), kc_count), :])

        def body(j, acc):
            return chunk(chunk(acc, 2 * j * kc_count), (2 * j + 1) * kc_count)

        acc = lax.fori_loop(0, n_cnt // 2, body, init)
        return lax.cond(n_cnt % 2 == 1, lambda a: chunk(a, (n_cnt - 1) * kc_count), lambda a: a, acc)

    def fold(x, op):
        return op(x.reshape(kc_count // SUBLANES, SUBLANES, tq), axis=0)

    def count_ge(t_row):
        acc = sweep(jnp.zeros((SUBLANES, tq), F32),
                    lambda a, s: a + fold(jnp.where(s >= t_row, 1.0, 0.0), jnp.sum))
        return jnp.sum(acc, axis=0, keepdims=True)

    def next_above(t_row):
        def body(acc, s):
            c, mn = acc
            gt = s > t_row
            return (c + fold(jnp.where(gt, 1.0, 0.0), jnp.sum),
                    jnp.minimum(mn, fold(jnp.where(gt, s, jnp.inf), jnp.min)))
        c, mn = sweep((jnp.zeros((SUBLANES, tq), F32), jnp.full((SUBLANES, tq), jnp.inf, F32)), body)
        return jnp.sum(c, axis=0, keepdims=True), jnp.min(mn, axis=0, keepdims=True)

    def active_queries(lo, hi, cnt_lo):
        return (cnt_lo > kf) & ((hi - lo) != 1)

    def n_true(mask):
        return jnp.sum(jnp.where(mask, 1.0, 0.0))

    few = qpos < kc_score
    lo0 = jnp.where(few, KEY_LOWEST, _f32_to_key(grp_min))
    hi0 = _f32_to_key(jnp.maximum(2.0 * row_max, TINY_NORMAL))
    cnt_lo0 = (qpos + 1).astype(F32)
    cnt_hi0 = jnp.zeros((1, tq), F32)

    def a_step(it, st):
        lo, hi, cnt_lo, cnt_hi, same = st
        act = active_queries(lo, hi, cnt_lo)
        key_mid = lo + lax.shift_right_logical(hi - lo, jnp.ones_like(lo))
        val_mid = 0.5 * _key_to_f32(lo) + 0.5 * _key_to_f32(hi)
        val_key = jnp.minimum(jnp.maximum(_f32_to_key(val_mid), lo + 1), hi - 1)
        use_val = (it % KEY_MID_EVERY != KEY_MID_EVERY - 1) & (jnp.abs(val_mid) < jnp.inf)
        mid = jnp.where(use_val, val_key, key_mid)
        c = count_ge(_key_to_f32(mid))
        ge = c >= kf
        take_lo = act & ge
        take_hi = act & jnp.logical_not(ge)
        new_lo = jnp.where(take_lo, c, cnt_lo)
        new_hi = jnp.where(take_hi, c, cnt_hi)
        same = jnp.where((new_lo - new_hi) == (cnt_lo - cnt_hi), same + 1.0, 0.0)
        return jnp.where(take_lo, mid, lo), jnp.where(take_hi, mid, hi), new_lo, new_hi, same

    def n_wide(st):
        lo, hi, cnt_lo, cnt_hi, same = st
        wide = (active_queries(lo, hi, cnt_lo) & ((cnt_lo - cnt_hi) > BRACKET_SMALL)
                & (same < STAGNANT_PASSES))
        return n_true(wide)

    def a_body(carry):
        _, it, st = carry
        st = a_step(it, st)
        return n_wide(st), it + 1, st

    st = (lo0, hi0, cnt_lo0, cnt_hi0, jnp.zeros((1, tq), F32))
    st = lax.fori_loop(0, BLIND_PASSES, a_step, st)
    _, _, (lo, hi, cnt_lo, cnt_hi, _) = lax.while_loop(
        lambda c: c[0] > 0.0, a_body, (n_wide(st), jnp.int32(BLIND_PASSES), st))

    def b_body(st):
        _, lo, cnt_lo, cnt_gt, fin = st
        pend = (cnt_lo > kf) & (fin == 0.0)
        c_gt, nxt = next_above(_key_to_f32(lo))
        done = pend & (c_gt < kf)
        step = pend & jnp.logical_not(done)
        cnt_gt = jnp.where(done, c_gt, cnt_gt)
        fin = jnp.where(done, 1.0, fin)
        lo = jnp.where(step, _f32_to_key(nxt), lo)
        cnt_lo = jnp.where(step, c_gt, cnt_lo)
        return (n_true(step & (cnt_lo > kf)), lo, cnt_lo, cnt_gt, fin)

    zeros = jnp.zeros((1, tq), F32)
    _, lo_b, cnt_lo_b, cnt_gt, _ = lax.while_loop(
        lambda st: st[0] > 0.0, b_body, (n_true(cnt_lo > kf), lo, cnt_lo, zeros, zeros))

    t_lo = _key_to_f32(lo_b)
    need = jnp.where(cnt_lo_b > kf, kf - cnt_gt, jnp.inf)

    accT_ref[...] = jnp.zeros(accT_ref.shape, F32)
    tri = (lax.broadcasted_iota(I32, (kc_attn, kc_attn), 0)
           >= lax.broadcasted_iota(I32, (kc_attn, kc_attn), 1)).astype(BF16)

    n_att = n_keys // kc_attn

    def select_bias(j, ties_seen):
        s = sT_ref[pl.ds(pl.multiple_of(j * kc_attn, kc_attn), kc_attn), :]
        is_eq = s == t_lo
        rank = jnp.dot(tri, jnp.where(is_eq, 1.0, 0.0).astype(BF16), preferred_element_type=F32)
        sel = (s > t_lo) | (is_eq & (rank + ties_seen <= need))
        return jnp.where(sel, 0.0, MASK_VAL), ties_seen + rank[kc_attn - 1:kc_attn, :]

    def logits_head(hh, j, dst_ref, bias):
        off = pl.multiple_of(j * kc_attn, kc_attn)
        pair = hh // 2
        kp = k_ref[0, pl.ds(off, kc_attn), pair * LANES:(pair + 1) * LANES]
        lg = jnp.dot(kp, qT_ref[0, hh * LANES:(hh + 1) * LANES, :], preferred_element_type=F32) + bias
        dst_ref[hh] = lg
        return jnp.max(lg, axis=0, keepdims=True)

    def softmax_head(hh, j, src_ref, m_new, alpha):
        off = pl.multiple_of(j * kc_attn, kc_attn)
        p = jnp.exp2(src_ref[hh] - m_new[hh:hh + 1, :])
        hs = slice(hh * HEAD_DIM, (hh + 1) * HEAD_DIM)
        pv = jnp.dot(vT_ref[0, hs, pl.ds(off, kc_attn)], p.astype(BF16), preferred_element_type=F32)
        accT_ref[hs, :] = alpha[hh:hh + 1, :] * accT_ref[hs, :] + pv
        return jnp.sum(p, axis=0, keepdims=True)

    def step(j, carry, src_ref, dst_ref):
        m_old, l_old, cmax, ties_seen = carry
        m_new = jnp.maximum(m_old, cmax)
        alpha = jnp.exp2(m_old - m_new)
        bias_next, ties_seen = select_bias(j + 1, ties_seen)
        psum, cmax_next = [], []
        for hh in range(N_HEADS):
            psum.append(softmax_head(hh, j, src_ref, m_new, alpha))
            cmax_next.append(logits_head(hh, j + 1, dst_ref, bias_next))
        return (m_new, alpha * l_old + jnp.concatenate(psum, axis=0),
                jnp.concatenate(cmax_next, axis=0), ties_seen)

    def last_step(j, carry, src_ref):
        m_old, l_old, cmax, _ = carry
        m_new = jnp.maximum(m_old, cmax)
        alpha = jnp.exp2(m_old - m_new)
        psum = [softmax_head(hh, j, src_ref, m_new, alpha) for hh in range(N_HEADS)]
        return alpha * l_old + jnp.concatenate(psum, axis=0)

    def chunk_pair(i, carry):
        carry = step(2 * i, carry, lga_ref, lgb_ref)
        return step(2 * i + 1, carry, lgb_ref, lga_ref)

    m0 = jnp.full((N_HEADS, tq), MASK_VAL, F32)
    l0 = jnp.zeros((N_HEADS, tq), F32)
    bias0, ties0 = select_bias(0, jnp.zeros((1, tq), F32))
    cmax0 = jnp.concatenate([logits_head(hh, 0, lga_ref, bias0) for hh in range(N_HEADS)], axis=0)
    n_pairs = (n_att - 1) // 2
    carry = lax.fori_loop(0, n_pairs, chunk_pair, (m0, l0, cmax0, ties0))
    j_rem = 2 * n_pairs

    def two_left(c):
        return last_step(j_rem + 1, step(j_rem, c, lga_ref, lgb_ref), lgb_ref)

    lsum = lax.cond(j_rem + 2 == n_att, two_left, lambda c: last_step(j_rem, c, lga_ref), carry)

    for hh in range(N_HEADS):
        hs = slice(hh * HEAD_DIM, (hh + 1) * HEAD_DIM)
        accT_ref[hs, :] = accT_ref[hs, :] / lsum[hh:hh + 1, :]
    o_ref[0] = accT_ref[...].T.astype(o_ref.dtype)


def _conv_kernel(x_ref, gmix_ref, wci_ref, cw_ref, cb_ref, lng_ref, lnb_ref, o_ref, ext_ref, sh_ref, y_ref):
    tm = x_ref.shape[1]
    i = pl.program_id(1)
    h = _rms_h(x_ref[0], gmix_ref[...])
    u = jnp.dot(h, wci_ref[...], preferred_element_type=F32)

    @pl.when(i == 0)
    def _():
        ext_ref[0:CONV_HALO, :] = jnp.zeros((CONV_HALO, CONV_CH), F32)

    ext_ref[CONV_HALO:CONV_HALO + tm, :] = u[:, :CONV_CH] * jax.nn.sigmoid(u[:, CONV_CH:])
    n_sh = tm + CONV_HALO - SUBLANES
    for b in range(1, SUBLANES):
        sh_ref[b - 1] = ext_ref[b:b + n_sh, :]

    base = CONV_HALO - (CONV_KERNEL - 1)

    def row_block(r, carry):
        r0 = pl.multiple_of(r * CONV_ROWS, CONV_ROWS)
        y = jnp.broadcast_to(cb_ref[...], (CONV_ROWS, CONV_CH))
        for j in range(CONV_KERNEL):
            a, b = divmod(base + j, SUBLANES)
            src = ext_ref if b == 0 else sh_ref.at[b - 1]
            y = y + cw_ref[j:j + 1, :] * src[pl.ds(r0 + a * SUBLANES, CONV_ROWS), :]
        y_ref[pl.ds(r0, CONV_ROWS), :] = y
        return carry

    lax.fori_loop(0, tm // CONV_ROWS, row_block, 0)
    y = y_ref[...]
    mu = jnp.mean(y, axis=-1, keepdims=True)
    yc = y - mu
    var = jnp.mean(yc * yc, axis=-1, keepdims=True)
    z = yc * lax.rsqrt(var + EPS) * lng_ref[...] + lnb_ref[...]
    o_ref[0] = (z * jax.nn.sigmoid(z)).astype(o_ref.dtype)
    ext_ref[0:CONV_HALO, :] = ext_ref[tm:tm + CONV_HALO, :]


def _mix_kernel(x_ref, attn_ref, conv_ref, p_ref, gmix_ref, wga_ref, wgc_ref, wao_ref, wco_ref, wout_ref,
                gffn_ref, wfg_ref, wfu_ref, wfd_ref, gple_ref, wpg_ref, wpp_ref, o_ref, *, ff_chunk):
    x = x_ref[...]
    h = _rms_h(x, gmix_ref[...])
    ga = jax.nn.sigmoid(jnp.dot(h, wga_ref[...], preferred_element_type=F32))
    gc = jax.nn.sigmoid(jnp.dot(h, wgc_ref[...], preferred_element_type=F32))
    merged = (ga * jnp.dot(attn_ref[...], wao_ref[...], preferred_element_type=F32)
              + gc * jnp.dot(conv_ref[...], wco_ref[...], preferred_element_type=F32))
    x = x + jnp.dot(merged.astype(BF16), wout_ref[...], preferred_element_type=F32)

    hf = _rms_h(x, gffn_ref[...])
    d_ff = wfg_ref.shape[1]
    ffn = jnp.zeros(x.shape, F32)
    for c in range(d_ff // ff_chunk):
        sl = slice(c * ff_chunk, (c + 1) * ff_chunk)
        g = jnp.dot(hf, wfg_ref[:, sl], preferred_element_type=F32)
        up = jnp.dot(hf, wfu_ref[:, sl], preferred_element_type=F32)
        act = (g * jax.nn.sigmoid(g) * up).astype(BF16)
        ffn = ffn + jnp.dot(act, wfd_ref[sl, :], preferred_element_type=F32)
    x = x + ffn

    hp = _rms_h(x, gple_ref[...])
    gate = jax.nn.sigmoid(jnp.dot(hp, wpg_ref[...], preferred_element_type=F32))
    emb = jnp.dot(p_ref[...].astype(BF16), wpp_ref[...], preferred_element_type=F32)
    o_ref[...] = x + gate * emb


def _const_spec(shape):
    nd = len(shape)
    return pl.BlockSpec(shape, lambda *_: (0,) * nd, pipeline_mode=pl.Buffered(1))


def _layer(x, p, g_mix, w_in, g_q, g_k, conv_w, conv_b, conv_ln_g, conv_ln_b, w_attn_o, w_conv_o, w_out,
           g_ffn, w_ffn_gate, w_ffn_up, w_ffn_down, g_ple, w_ple_gate, w_ple_proj):
    B, S, D = x.shape
    T = B * S
    topk = min(TOPK_MAX, S // 4)
    x2 = x.reshape(T, D)

    sizes = (ATTN_WIDTH, ATTN_WIDTH, ATTN_WIDTH, N_IDX_HEADS * IDX_DIM, IDX_DIM, N_IDX_HEADS,
             2 * CONV_CH, D, D)
    offs = [0]
    for s in sizes:
        offs.append(offs[-1] + s)
    wcols = [w_in[:, offs[i]:offs[i + 1]] for i in range(len(sizes))]
    wq, wk, wv, wqi, wki, wwi, wci, wga, wgc = wcols
    wqT = wq.T.astype(BF16)
    wk = wk.astype(BF16)
    wvT = wv.T.astype(BF16)
    wqiT = wqi.T.astype(BF16)
    wki2 = jnp.concatenate([wki, wki], axis=1).astype(BF16)
    wwiT = wwi.T.astype(BF16)
    gmix = g_mix.reshape(1, D)
    q_scale = (HEAD_DIM ** -0.5) * math.log2(math.e)
    gq = (jnp.tile(g_q, N_HEADS) * q_scale).reshape(ATTN_WIDTH, 1)
    gk = jnp.tile(g_k, N_HEADS).reshape(1, ATTN_WIDTH)
    head_id = jnp.arange(ATTN_WIDTH) // HEAD_DIM
    hd = (head_id[:, None] == head_id[None, :]).astype(BF16)

    tm = 512
    ns = S // tm
    row_tile = lambda i: (i, 0)
    seq_tile = lambda i: (i // ns, 0, i % ns)
    proj = pl.pallas_call(
        _proj_kernel,
        grid=(T // tm,),
        in_specs=[
            pl.BlockSpec((tm, D), row_tile),
            _const_spec((1, D)),
            _const_spec((ATTN_WIDTH, D)), _const_spec((D, ATTN_WIDTH)), _const_spec((ATTN_WIDTH, D)),
            _const_spec((ATTN_WIDTH, D)), _const_spec((D, LANES)), _const_spec((N_IDX_HEADS, D)),
            _const_spec((ATTN_WIDTH, 1)), _const_spec((1, ATTN_WIDTH)), _const_spec((ATTN_WIDTH, ATTN_WIDTH)),
        ],
        out_specs=[
            pl.BlockSpec((1, 2 * ATTN_WIDTH, tm), seq_tile),
            pl.BlockSpec((tm, ATTN_WIDTH), row_tile),
            pl.BlockSpec((1, ATTN_WIDTH, tm), seq_tile),
            pl.BlockSpec((1, 2 * ATTN_WIDTH, tm), seq_tile),
            pl.BlockSpec((tm, LANES), row_tile),
            pl.BlockSpec((1, N_IDX_HEADS, tm), seq_tile),
        ],
        out_shape=[
            jax.ShapeDtypeStruct((B, 2 * ATTN_WIDTH, S), BF16),
            jax.ShapeDtypeStruct((T, ATTN_WIDTH), BF16),
            jax.ShapeDtypeStruct((B, ATTN_WIDTH, S), BF16),
            jax.ShapeDtypeStruct((B, 2 * ATTN_WIDTH, S), BF16),
            jax.ShapeDtypeStruct((T, LANES), BF16),
            jax.ShapeDtypeStruct((B, N_IDX_HEADS, S), F32),
        ],
        compiler_params=pltpu.CompilerParams(dimension_semantics=("arbitrary",), vmem_limit_bytes=VMEM_LIMIT),
        name="dsa_proj",
    )
    qT, k, vT, qiT, ki2, wT = proj(x2, gmix, wqT, wk, wvT, wqiT, wki2, wwiT, gq, gk, hd)

    tq = 256
    kc_score = 256
    kc_attn = 256
    assert topk <= kc_score
    resident = functools.partial(pl.BlockSpec, pipeline_mode=pl.Buffered(1))
    dsa = pl.pallas_call(
        functools.partial(_dsa_kernel, topk=topk, kc_score=kc_score, kc_count=256, kc_attn=kc_attn),
        grid=(B, S // tq),
        in_specs=[
            pl.BlockSpec((1, 2 * ATTN_WIDTH, tq), lambda b, n: (b, 0, n)),
            pl.BlockSpec((1, 2 * ATTN_WIDTH, tq), lambda b, n: (b, 0, n)),
            pl.BlockSpec((1, N_IDX_HEADS, tq), lambda b, n: (b, 0, n)),
            resident((1, S, ATTN_WIDTH), lambda b, n: (b, 0, 0)),
            resident((1, S, LANES), lambda b, n: (b, 0, 0)),
            resident((1, ATTN_WIDTH, S), lambda b, n: (b, 0, 0)),
        ],
        out_specs=pl.BlockSpec((1, tq, ATTN_WIDTH), lambda b, n: (b, n, 0)),
        out_shape=jax.ShapeDtypeStruct((B, S, ATTN_WIDTH), BF16),
        scratch_shapes=[
            pltpu.VMEM((S, tq), F32),
            pltpu.VMEM((ATTN_WIDTH, tq), F32),
            pltpu.VMEM((N_HEADS, kc_attn, tq), F32),
            pltpu.VMEM((N_HEADS, kc_attn, tq), F32),
        ],
        compiler_params=pltpu.CompilerParams(dimension_semantics=("arbitrary", "arbitrary"),
                                             vmem_limit_bytes=VMEM_LIMIT),
        name="dsa_attn",
    )
    attn = dsa(qT, qiT, wT, k.reshape(B, S, ATTN_WIDTH), ki2.reshape(B, S, LANES), vT)

    tmc = 512
    cw = jnp.pad(conv_w.reshape(CONV_KERNEL, CONV_CH), ((0, CONV_HALO - CONV_KERNEL), (0, 0)))
    conv = pl.pallas_call(
        _conv_kernel,
        grid=(B, S // tmc),
        in_specs=[
            pl.BlockSpec((1, tmc, D), lambda b, i: (b, i, 0)),
            _const_spec((1, D)),
            _const_spec((D, 2 * CONV_CH)),
            _const_spec((CONV_HALO, CONV_CH)),
            _const_spec((1, CONV_CH)), _const_spec((1, CONV_CH)), _const_spec((1, CONV_CH)),
        ],
        out_specs=pl.BlockSpec((1, tmc, CONV_CH), lambda b, i: (b, i, 0)),
        out_shape=jax.ShapeDtypeStruct((B, S, CONV_CH), BF16),
        scratch_shapes=[pltpu.VMEM((tmc + CONV_HALO, CONV_CH), F32),
                        pltpu.VMEM((SUBLANES - 1, tmc + CONV_HALO - SUBLANES, CONV_CH), F32),
                        pltpu.VMEM((tmc, CONV_CH), F32)],
        compiler_params=pltpu.CompilerParams(dimension_semantics=("arbitrary", "arbitrary"),
                                             vmem_limit_bytes=VMEM_LIMIT),
        name="conv_module",
    )(x, gmix, wci.astype(BF16), cw, conv_b.reshape(1, CONV_CH), conv_ln_g.reshape(1, CONV_CH),
      conv_ln_b.reshape(1, CONV_CH))

    tmd = 512
    d_ff = w_ffn_gate.shape[1]
    ple = p.shape[-1]
    ff_chunk = 256 if d_ff % 256 == 0 else d_ff
    out = pl.pallas_call(
        functools.partial(_mix_kernel, ff_chunk=ff_chunk),
        grid=(T // tmd,),
        in_specs=[
            pl.BlockSpec((tmd, D), row_tile),
            pl.BlockSpec((tmd, ATTN_WIDTH), row_tile),
            pl.BlockSpec((tmd, CONV_CH), row_tile),
            pl.BlockSpec((tmd, ple), row_tile),
            _const_spec((1, D)),
            _const_spec((D, D)), _const_spec((D, D)),
            _const_spec((ATTN_WIDTH, D)), _const_spec((CONV_CH, D)), _const_spec((D, D)),
            _const_spec((1, D)),
            _const_spec((D, d_ff)), _const_spec((D, d_ff)), _const_spec((d_ff, D)),
            _const_spec((1, D)),
            _const_spec((D, D)), _const_spec((ple, D)),
        ],
        out_specs=pl.BlockSpec((tmd, D), row_tile),
        out_shape=jax.ShapeDtypeStruct((T, D), F32),
        compiler_params=pltpu.CompilerParams(dimension_semantics=("arbitrary",), vmem_limit_bytes=VMEM_LIMIT),
        name="mix_ffn_ple",
    )(x2, attn.reshape(T, ATTN_WIDTH), conv.reshape(T, CONV_CH), p.reshape(T, ple),
      gmix, wga.astype(BF16), wgc.astype(BF16), w_attn_o.astype(BF16), w_conv_o.astype(BF16),
      w_out.astype(BF16), g_ffn.reshape(1, D), w_ffn_gate.astype(BF16), w_ffn_up.astype(BF16),
      w_ffn_down.astype(BF16), g_ple.reshape(1, D), w_ple_gate.astype(BF16), w_ple_proj.astype(BF16))
    return out.reshape(B, S, D)


def kernel(x, p, g_mix, w_in, g_q, g_k, conv_w, conv_b, conv_ln_g, conv_ln_b, w_attn_o, w_conv_o, w_out,
           g_ffn, w_ffn_gate, w_ffn_up, w_ffn_down, g_ple, w_ple_gate, w_ple_proj):
    depth = w_in.shape[0]
    for i in range(depth):
        x = _layer(x, p[i], g_mix[i], w_in[i], g_q[i], g_k[i], conv_w[i], conv_b[i], conv_ln_g[i],
                   conv_ln_b[i], w_attn_o[i], w_conv_o[i], w_out[i], g_ffn[i], w_ffn_gate[i],
                   w_ffn_up[i], w_ffn_down[i], g_ple[i], w_ple_gate[i], w_ple_proj[i])
    return x
```

```python
import functools
import math

import jax
import jax.numpy as jnp
from jax import lax
from jax.experimental import pallas as pl
from jax.experimental.pallas import tpu as pltpu

F32 = jnp.float32
BF16 = jnp.bfloat16
I32 = jnp.int32

N_HEADS = 8
HEAD_DIM = 64
ATTN_WIDTH = N_HEADS * HEAD_DIM
N_IDX_HEADS = 8
IDX_DIM = 64
TOPK_MAX = 256
CONV_CH = 512
CONV_KERNEL = 31
EPS = 1e-6

LANES = 128
SUBLANES = 8
BF16_ROWS = 16
CONV_HALO = 32
CONV_ROWS = 64
VMEM_LIMIT = 56 * 1024 * 1024

KEY_LOWEST = -2139095040
KEY_INF = 2139095040
MASK_VAL = -1e30
TINY_NORMAL = 1e-30
BRACKET_SMALL = 4.0
STAGNANT_PASSES = 2.0
BLIND_PASSES = 10
KEY_MID_EVERY = 4

NT_DIMS = (((1,), (1,)), ((), ()))


def _key_to_f32(k):
    bits = k ^ ((k >> 31) & 0x7FFFFFFF)
    return lax.bitcast_convert_type(bits, F32)


def _f32_to_key(x):
    bits = lax.bitcast_convert_type(x, I32)
    return bits ^ ((bits >> 31) & 0x7FFFFFFF)


def _rms_h(x, g):
    ms = jnp.mean(x * x, axis=-1, keepdims=True)
    return (x * lax.rsqrt(ms + EPS) * g).astype(BF16)


def _split_bf16(a):
    hi = a.astype(BF16)
    lo = (a - hi.astype(F32)).astype(BF16)
    return hi, lo


def _proj_kernel(x_ref, gmix_ref, wqT_ref, wk_ref, wvT_ref, wqiT_ref, wki_ref, wwiT_ref,
                 gq_ref, gk_ref, hd_ref,
                 qT_out, k_out, vT_out, qiT_out, ki_out, wT_out):
    tm = x_ref.shape[0]
    h = _rms_h(x_ref[...], gmix_ref[...])
    hd = hd_ref[...]

    row = lax.broadcasted_iota(I32, (LANES, tm), 0)
    top_half = row < HEAD_DIM

    def expand_heads(yT, out_ref):
        for pair in range(N_HEADS // 2):
            blk = yT[pair * LANES:(pair + 1) * LANES, :]
            out_ref[0, (2 * pair) * LANES:(2 * pair + 1) * LANES, :] = jnp.where(top_half, blk, 0.0).astype(BF16)
            out_ref[0, (2 * pair + 1) * LANES:(2 * pair + 2) * LANES, :] = jnp.where(top_half, 0.0, blk).astype(BF16)

    qT = lax.dot_general(wqT_ref[...], h, NT_DIMS, preferred_element_type=F32)
    hi, lo = _split_bf16(qT * qT)
    ssq = jnp.dot(hd, hi, preferred_element_type=F32) + jnp.dot(hd, lo, preferred_element_type=F32)
    expand_heads(qT * lax.rsqrt(ssq * (1.0 / HEAD_DIM) + EPS) * gq_ref[...], qT_out)

    k = jnp.dot(h, wk_ref[...], preferred_element_type=F32)
    hi, lo = _split_bf16(k * k)
    ssk = jnp.dot(hi, hd, preferred_element_type=F32) + jnp.dot(lo, hd, preferred_element_type=F32)
    k_out[...] = (k * lax.rsqrt(ssk * (1.0 / HEAD_DIM) + EPS) * gk_ref[...]).astype(BF16)

    vT_out[0] = lax.dot_general(wvT_ref[...], h, NT_DIMS, preferred_element_type=F32).astype(BF16)

    qiT = lax.dot_general(wqiT_ref[...], h, NT_DIMS, preferred_element_type=F32) * (IDX_DIM ** -0.5)
    expand_heads(qiT, qiT_out)

    ki_out[...] = jnp.dot(h, wki_ref[...], preferred_element_type=F32).astype(BF16)
    wT_out[0] = (lax.dot_general(wwiT_ref[...], h, NT_DIMS, preferred_element_type=F32)
                 * (N_IDX_HEADS ** -0.5))


def _dsa_kernel(qT_ref, qiT_ref, wT_ref, k_ref, ki_ref, vT_ref, o_ref,
                sT_ref, accT_ref, lga_ref, lgb_ref, *, topk, kc_score, kc_count, kc_attn):
    tq = qT_ref.shape[2]
    n = pl.program_id(1)
    t0 = n * tq
    n_keys = t0 + tq
    qpos = t0 + lax.broadcasted_iota(I32, (1, tq), 1)
    kf = float(topk)

    def score_rows(off, rows, masked):
        ki = ki_ref[0, pl.ds(off, rows), :]
        acc = jnp.zeros((rows, tq), F32)
        for hh in range(N_IDX_HEADS):
            z = jnp.dot(ki, qiT_ref[0, hh * LANES:(hh + 1) * LANES, :], preferred_element_type=F32)
            acc = acc + jnp.maximum(z, 0.0) * wT_ref[0, hh:hh + 1, :]
        if masked:
            kpos = off + lax.broadcasted_iota(I32, (rows, tq), 0)
            acc = jnp.where(kpos <= qpos, acc, -jnp.inf)
        sT_ref[pl.ds(off, rows), :] = acc
        return acc

    def big_chunk(j, g):
        acc = score_rows(pl.multiple_of(j * 2 * kc_score, 2 * kc_score), 2 * kc_score, False)
        return jnp.maximum(g, jnp.maximum(acc[:kc_score], acc[kc_score:]))

    n_big = t0 // (2 * kc_score)
    g = lax.fori_loop(0, n_big, big_chunk, jnp.full((kc_score, tq), -jnp.inf, F32))
    rem_off = pl.multiple_of(n_big * 2 * kc_score, kc_score)
    g = lax.cond(rem_off < t0, lambda g: jnp.maximum(g, score_rows(rem_off, kc_score, False)), lambda g: g, g)
    for d in range(tq // kc_score):
        g = jnp.maximum(g, score_rows(pl.multiple_of(t0 + d * kc_score, kc_score), kc_score, True))
    grp_min = jnp.min(g, axis=0, keepdims=True)
    row_max = jnp.max(g, axis=0, keepdims=True)

    n_cnt = n_keys // kc_count

    def sweep(init, body_fn):
        def chunk(acc, off):
            return body_fn(acc, sT_ref[pl.ds(pl.multiple_of(off, kc_count), kc_count), :])

        def body(j, acc):
            return chunk(chunk(acc, 2 * j * kc_count), (2 * j + 1) * kc_count)

        acc = lax.fori_loop(0, n_cnt // 2, body, init)
        return lax.cond(n_cnt % 2 == 1, lambda a: chunk(a, (n_cnt - 1) * kc_count), lambda a: a, acc)

    def fold(x, op):
        return op(x.reshape(kc_count // SUBLANES, SUBLANES, tq), axis=0)

    def count_ge(t_row):
        acc = sweep(jnp.zeros((SUBLANES, tq), F32),
                    lambda a, s: a + fold(jnp.where(s >= t_row, 1.0, 0.0), jnp.sum))
        return jnp.sum(acc, axis=0, keepdims=True)

    def next_above(t_row):
        def body(acc, s):
            c, mn = acc
            gt = s > t_row
            return (c + fold(jnp.where(gt, 1.0, 0.0), jnp.sum),
                    jnp.minimum(mn, fold(jnp.where(gt, s, jnp.inf), jnp.min)))
        c, mn = sweep((jnp.zeros((SUBLANES, tq), F32), jnp.full((SUBLANES, tq), jnp.inf, F32)), body)
        return jnp.sum(c, axis=0, keepdims=True), jnp.min(mn, axis=0, keepdims=True)

    def active_queries(lo, hi, cnt_lo):
        return (cnt_lo > kf) & ((hi - lo) != 1)

    def n_true(mask):
        return jnp.sum(jnp.where(mask, 1.0, 0.0))

    few = qpos < kc_score
    lo0 = jnp.where(few, KEY_LOWEST, _f32_to_key(grp_min))
    hi0 = _f32_to_key(jnp.maximum(2.0 * row_max, TINY_NORMAL))
    cnt_lo0 = (qpos + 1).astype(F32)
    cnt_hi0 = jnp.zeros((1, tq), F32)

    def a_step(it, st):
        lo, hi, cnt_lo, cnt_hi, same = st
        act = active_queries(lo, hi, cnt_lo)
        key_mid = lo + lax.shift_right_logical(hi - lo, jnp.ones_like(lo))
        val_mid = 0.5 * _key_to_f32(lo) + 0.5 * _key_to_f32(hi)
        val_key = jnp.minimum(jnp.maximum(_f32_to_key(val_mid), lo + 1), hi - 1)
        use_val = (it % KEY_MID_EVERY != KEY_MID_EVERY - 1) & (jnp.abs(val_mid) < jnp.inf)
        mid = jnp.where(use_val, val_key, key_mid)
        c = count_ge(_key_to_f32(mid))
        ge = c >= kf
        take_lo = act & ge
        take_hi = act & jnp.logical_not(ge)
        new_lo = jnp.where(take_lo, c, cnt_lo)
        new_hi = jnp.where(take_hi, c, cnt_hi)
        same = jnp.where((new_lo - new_hi) == (cnt_lo - cnt_hi), same + 1.0, 0.0)
        return jnp.where(take_lo, mid, lo), jnp.where(take_hi, mid, hi), new_lo, new_hi, same

    def n_wide(st):
        lo, hi, cnt_lo, cnt_hi, same = st
        wide = (active_queries(lo, hi, cnt_lo) & ((cnt_lo - cnt_hi) > BRACKET_SMALL)
                & (same < STAGNANT_PASSES))
        return n_true(wide)

    def a_body(carry):
        _, it, st = carry
        st = a_step(it, st)
        return n_wide(st), it + 1, st

    st = (lo0, hi0, cnt_lo0, cnt_hi0, jnp.zeros((1, tq), F32))
    st = lax.fori_loop(0, BLIND_PASSES, a_step, st)
    _, _, (lo, hi, cnt_lo, cnt_hi, _) = lax.while_loop(
        lambda c: c[0] > 0.0, a_body, (n_wide(st), jnp.int32(BLIND_PASSES), st))

    def b_body(st):
        _, lo, cnt_lo, cnt_gt, fin = st
        pend = (cnt_lo > kf) & (fin == 0.0)
        c_gt, nxt = next_above(_key_to_f32(lo))
        done = pend & (c_gt < kf)
        step = pend & jnp.logical_not(done)
        cnt_gt = jnp.where(done, c_gt, cnt_gt)
        fin = jnp.where(done, 1.0, fin)
        lo = jnp.where(step, _f32_to_key(nxt), lo)
        cnt_lo = jnp.where(step, c_gt, cnt_lo)
        return (n_true(step & (cnt_lo > kf)), lo, cnt_lo, cnt_gt, fin)

    zeros = jnp.zeros((1, tq), F32)
    _, lo_b, cnt_lo_b, cnt_gt, _ = lax.while_loop(
        lambda st: st[0] > 0.0, b_body, (n_true(cnt_lo > kf), lo, cnt_lo, zeros, zeros))

    t_lo = _key_to_f32(lo_b)
    need = jnp.where(cnt_lo_b > kf, kf - cnt_gt, jnp.inf)

    accT_ref[...] = jnp.zeros(accT_ref.shape, F32)
    tri = (lax.broadcasted_iota(I32, (kc_attn, kc_attn), 0)
           >= lax.broadcasted_iota(I32, (kc_attn, kc_attn), 1)).astype(BF16)

    n_att = n_keys // kc_attn
    ones_rows = jnp.ones((BF16_ROWS, kc_attn), BF16)

    def select_bias(j, ties_seen):
        s = sT_ref[pl.ds(pl.multiple_of(j * kc_attn, kc_attn), kc_attn), :]
        is_eq = s == t_lo
        rank = jnp.dot(tri, jnp.where(is_eq, 1.0, 0.0).astype(BF16), preferred_element_type=F32)
        sel = (s > t_lo) | (is_eq & (rank + ties_seen <= need))
        return jnp.where(sel, 0.0, MASK_VAL), ties_seen + rank[kc_attn - 1:kc_attn, :]

    def logits_head(hh, j, dst_ref, bias):
        off = pl.multiple_of(j * kc_attn, kc_attn)
        pair = hh // 2
        kp = k_ref[0, pl.ds(off, kc_attn), pair * LANES:(pair + 1) * LANES]
        lg = jnp.dot(kp, qT_ref[0, hh * LANES:(hh + 1) * LANES, :], preferred_element_type=F32) + bias
        dst_ref[hh] = lg
        return jnp.max(lg, axis=0, keepdims=True)

    def softmax_head(hh, j, src_ref, m_new, alpha):
        off = pl.multiple_of(j * kc_attn, kc_attn)
        p = jnp.exp2(src_ref[hh] - m_new[hh:hh + 1, :]).astype(BF16)
        hs = slice(hh * HEAD_DIM, (hh + 1) * HEAD_DIM)
        v_ones = jnp.concatenate([vT_ref[0, hs, pl.ds(off, kc_attn)], ones_rows], axis=0)
        pv = jnp.dot(v_ones, p, preferred_element_type=F32)
        accT_ref[hs, :] = alpha[hh:hh + 1, :] * accT_ref[hs, :] + pv[:HEAD_DIM, :]
        return pv[HEAD_DIM:HEAD_DIM + 1, :]

    def step(j, carry, src_ref, dst_ref):
        m_old, l_old, cmax, ties_seen = carry
        m_new = jnp.maximum(m_old, cmax)
        alpha = jnp.exp2(m_old - m_new)
        bias_next, ties_seen = select_bias(j + 1, ties_seen)
        psum, cmax_next = [], []
        for hh in range(N_HEADS):
            psum.append(softmax_head(hh, j, src_ref, m_new, alpha))
            cmax_next.append(logits_head(hh, j + 1, dst_ref, bias_next))
        return (m_new, alpha * l_old + jnp.concatenate(psum, axis=0),
                jnp.concatenate(cmax_next, axis=0), ties_seen)

    def last_step(j, carry, src_ref):
        m_old, l_old, cmax, _ = carry
        m_new = jnp.maximum(m_old, cmax)
        alpha = jnp.exp2(m_old - m_new)
        psum = [softmax_head(hh, j, src_ref, m_new, alpha) for hh in range(N_HEADS)]
        return alpha * l_old + jnp.concatenate(psum, axis=0)

    def chunk_pair(i, carry):
        carry = step(2 * i, carry, lga_ref, lgb_ref)
        return step(2 * i + 1, carry, lgb_ref, lga_ref)

    m0 = jnp.full((N_HEADS, tq), MASK_VAL, F32)
    l0 = jnp.zeros((N_HEADS, tq), F32)
    bias0, ties0 = select_bias(0, jnp.zeros((1, tq), F32))
    cmax0 = jnp.concatenate([logits_head(hh, 0, lga_ref, bias0) for hh in range(N_HEADS)], axis=0)
    n_pairs = (n_att - 1) // 2
    carry = lax.fori_loop(0, n_pairs, chunk_pair, (m0, l0, cmax0, ties0))
    j_rem = 2 * n_pairs

    def two_left(c):
        return last_step(j_rem + 1, step(j_rem, c, lga_ref, lgb_ref), lgb_ref)

    lsum = lax.cond(j_rem + 2 == n_att, two_left, lambda c: last_step(j_rem, c, lga_ref), carry)

    for hh in range(N_HEADS):
        hs = slice(hh * HEAD_DIM, (hh + 1) * HEAD_DIM)
        accT_ref[hs, :] = accT_ref[hs, :] / lsum[hh:hh + 1, :]
    o_ref[0] = accT_ref[...].T.astype(o_ref.dtype)


def _conv_kernel(x_ref, gmix_ref, wci_ref, cw_ref, cb_ref, lng_ref, lnb_ref, o_ref, ext_ref, sh_ref, y_ref):
    tm = x_ref.shape[1]
    i = pl.program_id(1)
    h = _rms_h(x_ref[0], gmix_ref[...])
    u = jnp.dot(h, wci_ref[...], preferred_element_type=F32)

    @pl.when(i == 0)
    def _():
        ext_ref[0:CONV_HALO, :] = jnp.zeros((CONV_HALO, CONV_CH), F32)

    ext_ref[CONV_HALO:CONV_HALO + tm, :] = u[:, :CONV_CH] * jax.nn.sigmoid(u[:, CONV_CH:])
    n_sh = tm + CONV_HALO - SUBLANES
    for b in range(1, SUBLANES):
        sh_ref[b - 1] = ext_ref[b:b + n_sh, :]

    base = CONV_HALO - (CONV_KERNEL - 1)

    def row_block(r, carry):
        r0 = pl.multiple_of(r * CONV_ROWS, CONV_ROWS)
        y = jnp.broadcast_to(cb_ref[...], (CONV_ROWS, CONV_CH))
        for j in range(CONV_KERNEL):
            a, b = divmod(base + j, SUBLANES)
            src = ext_ref if b == 0 else sh_ref.at[b - 1]
            y = y + cw_ref[j:j + 1, :] * src[pl.ds(r0 + a * SUBLANES, CONV_ROWS), :]
        y_ref[pl.ds(r0, CONV_ROWS), :] = y
        return carry

    lax.fori_loop(0, tm // CONV_ROWS, row_block, 0)
    y = y_ref[...]
    mu = jnp.mean(y, axis=-1, keepdims=True)
    yc = y - mu
    var = jnp.mean(yc * yc, axis=-1, keepdims=True)
    z = yc * lax.rsqrt(var + EPS) * lng_ref[...] + lnb_ref[...]
    o_ref[0] = (z * jax.nn.sigmoid(z)).astype(o_ref.dtype)
    ext_ref[0:CONV_HALO, :] = ext_ref[tm:tm + CONV_HALO, :]


def _mix_kernel(x_ref, attn_ref, conv_ref, p_ref, gmix_ref, wga_ref, wgc_ref, wao_ref, wco_ref, wout_ref,
                gffn_ref, wfg_ref, wfu_ref, wfd_ref, gple_ref, wpg_ref, wpp_ref, o_ref, *, ff_chunk):
    x = x_ref[...]
    h = _rms_h(x, gmix_ref[...])
    ga = jax.nn.sigmoid(jnp.dot(h, wga_ref[...], preferred_element_type=F32))
    gc = jax.nn.sigmoid(jnp.dot(h, wgc_ref[...], preferred_element_type=F32))
    merged = (ga * jnp.dot(attn_ref[...], wao_ref[...], preferred_element_type=F32)
              + gc * jnp.dot(conv_ref[...], wco_ref[...], preferred_element_type=F32))
    x = x + jnp.dot(merged.astype(BF16), wout_ref[...], preferred_element_type=F32)

    hf = _rms_h(x, gffn_ref[...])
    d_ff = wfg_ref.shape[1]
    ffn = jnp.zeros(x.shape, F32)
    for c in range(d_ff // ff_chunk):
        sl = slice(c * ff_chunk, (c + 1) * ff_chunk)
        g = jnp.dot(hf, wfg_ref[:, sl], preferred_element_type=F32)
        up = jnp.dot(hf, wfu_ref[:, sl], preferred_element_type=F32)
        act = (g * jax.nn.sigmoid(g) * up).astype(BF16)
        ffn = ffn + jnp.dot(act, wfd_ref[sl, :], preferred_element_type=F32)
    x = x + ffn

    hp = _rms_h(x, gple_ref[...])
    gate = jax.nn.sigmoid(jnp.dot(hp, wpg_ref[...], preferred_element_type=F32))
    emb = jnp.dot(p_ref[...].astype(BF16), wpp_ref[...], preferred_element_type=F32)
    o_ref[...] = x + gate * emb


def _const_spec(shape):
    nd = len(shape)
    return pl.BlockSpec(shape, lambda *_: (0,) * nd, pipeline_mode=pl.Buffered(1))


def _layer(x, p, g_mix, w_in, g_q, g_k, conv_w, conv_b, conv_ln_g, conv_ln_b, w_attn_o, w_conv_o, w_out,
           g_ffn, w_ffn_gate, w_ffn_up, w_ffn_down, g_ple, w_ple_gate, w_ple_proj):
    B, S, D = x.shape
    T = B * S
    topk = min(TOPK_MAX, S // 4)
    x2 = x.reshape(T, D)

    sizes = (ATTN_WIDTH, ATTN_WIDTH, ATTN_WIDTH, N_IDX_HEADS * IDX_DIM, IDX_DIM, N_IDX_HEADS,
             2 * CONV_CH, D, D)
    offs = [0]
    for s in sizes:
        offs.append(offs[-1] + s)
    wcols = [w_in[:, offs[i]:offs[i + 1]] for i in range(len(sizes))]
    wq, wk, wv, wqi, wki, wwi, wci, wga, wgc = wcols
    wqT = wq.T.astype(BF16)
    wk = wk.astype(BF16)
    wvT = wv.T.astype(BF16)
    wqiT = wqi.T.astype(BF16)
    wki2 = jnp.concatenate([wki, wki], axis=1).astype(BF16)
    wwiT = wwi.T.astype(BF16)
    gmix = g_mix.reshape(1, D)
    q_scale = (HEAD_DIM ** -0.5) * math.log2(math.e)
    gq = (jnp.tile(g_q, N_HEADS) * q_scale).reshape(ATTN_WIDTH, 1)
    gk = jnp.tile(g_k, N_HEADS).reshape(1, ATTN_WIDTH)
    head_id = jnp.arange(ATTN_WIDTH) // HEAD_DIM
    hd = (head_id[:, None] == head_id[None, :]).astype(BF16)

    tm = 512
    ns = S // tm
    row_tile = lambda i: (i, 0)
    seq_tile = lambda i: (i // ns, 0, i % ns)
    proj = pl.pallas_call(
        _proj_kernel,
        grid=(T // tm,),
        in_specs=[
            pl.BlockSpec((tm, D), row_tile),
            _const_spec((1, D)),
            _const_spec((ATTN_WIDTH, D)), _const_spec((D, ATTN_WIDTH)), _const_spec((ATTN_WIDTH, D)),
            _const_spec((ATTN_WIDTH, D)), _const_spec((D, LANES)), _const_spec((N_IDX_HEADS, D)),
            _const_spec((ATTN_WIDTH, 1)), _const_spec((1, ATTN_WIDTH)), _const_spec((ATTN_WIDTH, ATTN_WIDTH)),
        ],
        out_specs=[
            pl.BlockSpec((1, 2 * ATTN_WIDTH, tm), seq_tile),
            pl.BlockSpec((tm, ATTN_WIDTH), row_tile),
            pl.BlockSpec((1, ATTN_WIDTH, tm), seq_tile),
            pl.BlockSpec((1, 2 * ATTN_WIDTH, tm), seq_tile),
            pl.BlockSpec((tm, LANES), row_tile),
            pl.BlockSpec((1, N_IDX_HEADS, tm), seq_tile),
        ],
        out_shape=[
            jax.ShapeDtypeStruct((B, 2 * ATTN_WIDTH, S), BF16),
            jax.ShapeDtypeStruct((T, ATTN_WIDTH), BF16),
            jax.ShapeDtypeStruct((B, ATTN_WIDTH, S), BF16),
            jax.ShapeDtypeStruct((B, 2 * ATTN_WIDTH, S), BF16),
            jax.ShapeDtypeStruct((T, LANES), BF16),
            jax.ShapeDtypeStruct((B, N_IDX_HEADS, S), F32),
        ],
        compiler_params=pltpu.CompilerParams(dimension_semantics=("arbitrary",), vmem_limit_bytes=VMEM_LIMIT),
        name="dsa_proj",
    )
    qT, k, vT, qiT, ki2, wT = proj(x2, gmix, wqT, wk, wvT, wqiT, wki2, wwiT, gq, gk, hd)

    tq = 256
    kc_score = 256
    kc_attn = 256
    assert topk <= kc_score
    resident = functools.partial(pl.BlockSpec, pipeline_mode=pl.Buffered(1))
    dsa = pl.pallas_call(
        functools.partial(_dsa_kernel, topk=topk, kc_score=kc_score, kc_count=256, kc_attn=kc_attn),
        grid=(B, S // tq),
        in_specs=[
            pl.BlockSpec((1, 2 * ATTN_WIDTH, tq), lambda b, n: (b, 0, n)),
            pl.BlockSpec((1, 2 * ATTN_WIDTH, tq), lambda b, n: (b, 0, n)),
            pl.BlockSpec((1, N_IDX_HEADS, tq), lambda b, n: (b, 0, n)),
            resident((1, S, ATTN_WIDTH), lambda b, n: (b, 0, 0)),
            resident((1, S, LANES), lambda b, n: (b, 0, 0)),
            resident((1, ATTN_WIDTH, S), lambda b, n: (b, 0, 0)),
        ],
        out_specs=pl.BlockSpec((1, tq, ATTN_WIDTH), lambda b, n: (b, n, 0)),
        out_shape=jax.ShapeDtypeStruct((B, S, ATTN_WIDTH), BF16),
        scratch_shapes=[
            pltpu.VMEM((S, tq), F32),
            pltpu.VMEM((ATTN_WIDTH, tq), F32),
            pltpu.VMEM((N_HEADS, kc_attn, tq), F32),
            pltpu.VMEM((N_HEADS, kc_attn, tq), F32),
        ],
        compiler_params=pltpu.CompilerParams(dimension_semantics=("arbitrary", "arbitrary"),
                                             vmem_limit_bytes=VMEM_LIMIT),
        name="dsa_attn",
    )
    attn = dsa(qT, qiT, wT, k.reshape(B, S, ATTN_WIDTH), ki2.reshape(B, S, LANES), vT)

    tmc = 512
    cw = jnp.pad(conv_w.reshape(CONV_KERNEL, CONV_CH), ((0, CONV_HALO - CONV_KERNEL), (0, 0)))
    conv = pl.pallas_call(
        _conv_kernel,
        grid=(B, S // tmc),
        in_specs=[
            pl.BlockSpec((1, tmc, D), lambda b, i: (b, i, 0)),
            _const_spec((1, D)),
            _const_spec((D, 2 * CONV_CH)),
            _const_spec((CONV_HALO, CONV_CH)),
            _const_spec((1, CONV_CH)), _const_spec((1, CONV_CH)), _const_spec((1, CONV_CH)),
        ],
        out_specs=pl.BlockSpec((1, tmc, CONV_CH), lambda b, i: (b, i, 0)),
        out_shape=jax.ShapeDtypeStruct((B, S, CONV_CH), BF16),
        scratch_shapes=[pltpu.VMEM((tmc + CONV_HALO, CONV_CH), F32),
                        pltpu.VMEM((SUBLANES - 1, tmc + CONV_HALO - SUBLANES, CONV_CH), F32),
                        pltpu.VMEM((tmc, CONV_CH), F32)],
        compiler_params=pltpu.CompilerParams(dimension_semantics=("arbitrary", "arbitrary"),
                                             vmem_limit_bytes=VMEM_LIMIT),
        name="conv_module",
    )(x, gmix, wci.astype(BF16), cw, conv_b.reshape(1, CONV_CH), conv_ln_g.reshape(1, CONV_CH),
      conv_ln_b.reshape(1, CONV_CH))

    tmd = 512
    d_ff = w_ffn_gate.shape[1]
    ple = p.shape[-1]
    ff_chunk = 256 if d_ff % 256 == 0 else d_ff
    out = pl.pallas_call(
        functools.partial(_mix_kernel, ff_chunk=ff_chunk),
        grid=(T // tmd,),
        in_specs=[
            pl.BlockSpec((tmd, D), row_tile),
            pl.BlockSpec((tmd, ATTN_WIDTH), row_tile),
            pl.BlockSpec((tmd, CONV_CH), row_tile),
            pl.BlockSpec((tmd, ple), row_tile),
            _const_spec((1, D)),
            _const_spec((D, D)), _const_spec((D, D)),
            _const_spec((ATTN_WIDTH, D)), _const_spec((CONV_CH, D)), _const_spec((D, D)),
            _const_spec((1, D)),
            _const_spec((D, d_ff)), _const_spec((D, d_ff)), _const_spec((d_ff, D)),
            _const_spec((1, D)),
            _const_spec((D, D)), _const_spec((ple, D)),
        ],
        out_specs=pl.BlockSpec((tmd, D), row_tile),
        out_shape=jax.ShapeDtypeStruct((T, D), F32),
        compiler_params=pltpu.CompilerParams(dimension_semantics=("arbitrary",), vmem_limit_bytes=VMEM_LIMIT),
        name="mix_ffn_ple",
    )(x2, attn.reshape(T, ATTN_WIDTH), conv.reshape(T, CONV_CH), p.reshape(T, ple),
      gmix, wga.astype(BF16), wgc.astype(BF16), w_attn_o.astype(BF16), w_conv_o.astype(BF16),
      w_out.astype(BF16), g_ffn.reshape(1, D), w_ffn_gate.astype(BF16), w_ffn_up.astype(BF16),
      w_ffn_down.astype(BF16), g_ple.reshape(1, D), w_ple_gate.astype(BF16), w_ple_proj.astype(BF16))
    return out.reshape(B, S, D)


def kernel(x, p, g_mix, w_in, g_q, g_k, conv_w, conv_b, conv_ln_g, conv_ln_b, w_attn_o, w_conv_o, w_out,
           g_ffn, w_ffn_gate, w_ffn_up, w_ffn_down, g_ple, w_ple_gate, w_ple_proj):
    depth = w_in.shape[0]
    for i in range(depth):
        x = _layer(x, p[i], g_mix[i], w_in[i], g_q[i], g_k[i], conv_w[i], conv_b[i], conv_ln_g[i],
                   conv_ln_b[i], w_attn_o[i], w_conv_o[i], w_out[i], g_ffn[i], w_ffn_gate[i],
                   w_ffn_up[i], w_ffn_down[i], g_ple[i], w_ple_gate[i], w_ple_proj[i])
    return x
```
